```python
import jax, jax.numpy as jnp
from jax import lax
import numpy as np

D_MODEL = 4096
BATCH = 1
SEQ = 8192
DEPTH = 2
DEC_BATCH = 32
DEC_SEQ = 64
PAST_LEN = 2048

CHUNK = 64
N_PAST_CHUNKS = 8
BAND_PAST = N_PAST_CHUNKS * CHUNK
BAND = (N_PAST_CHUNKS + 1) * CHUNK
D_MIX = D_MODEL
D_ATTN = D_MIX // 2
D_CONV = D_MIX - D_ATTN
N_HEADS = 16
HEAD_DIM = D_ATTN // N_HEADS
CONV_WIDTH = 31
CONV_STATE = CONV_WIDTH - 1
MAX_REL = 128
N_REL = 2 * MAX_REL + 1
D_FF = ((8 * D_MODEL + 3 * 256 - 1) // (3 * 256)) * 256
N_MOD = 6
D_IN = 3 * D_ATTN + 2 * D_CONV
EPS = 1e-6
NEG_INF = -1e30

kernel_name = "hybrid_conformer_stream_step"


def rms_norm(x, g):
    x32 = x.astype(jnp.float32)
    y = x32 * lax.rsqrt(jnp.mean(x32 * x32, axis=-1, keepdims=True) + EPS)
    return (y * g.astype(jnp.float32)).astype(x.dtype)


def layer_norm(x, g, b):
    x32 = x.astype(jnp.float32)
    mu = jnp.mean(x32, axis=-1, keepdims=True)
    xc = x32 - mu
    y = xc * lax.rsqrt(jnp.mean(xc * xc, axis=-1, keepdims=True) + EPS)
    return (y * g.astype(jnp.float32) + b.astype(jnp.float32)).astype(x.dtype)


def modulation(c, w_ada, b_ada):
    m = jax.nn.silu(c) @ w_ada + b_ada
    return jnp.split(m[:, None, :], N_MOD, axis=-1)


def project(h, w_in):
    B, T, _ = h.shape
    proj = h @ w_in
    q, k, v, ga, gb = jnp.split(proj, [D_ATTN, 2 * D_ATTN, 3 * D_ATTN, 3 * D_ATTN + D_CONV], axis=-1)
    shp = (B, T, N_HEADS, HEAD_DIM)
    u = ga * jax.nn.sigmoid(gb)
    return q.reshape(shp), k.reshape(shp), v.reshape(shp), u


def rel_bias_table(rel_bias_l, rel):
    idx = jnp.clip(rel, -MAX_REL, MAX_REL) + MAX_REL
    return rel_bias_l[:, idx]


def band_attention(q, k, v, bias, valid):
    s = jnp.einsum('bcqhd,bckhd->bchqk', q, k).astype(jnp.float32) * (HEAD_DIM ** -0.5)
    s = s + bias.astype(jnp.float32)
    if valid is not None:
        s = jnp.where(valid, s, NEG_INF)
    p = jax.nn.softmax(s, axis=-1).astype(v.dtype)
    return jnp.einsum('bchqk,bckhd->bcqhd', p, v)


def conv_branch(u_pad, w_dw, b_dw, g_ln, b_ln):
    y = lax.conv_general_dilated(u_pad, w_dw[:, None, :], window_strides=(1,), padding='VALID',
                                 dimension_numbers=('NWC', 'WIO', 'NWC'),
                                 feature_group_count=u_pad.shape[-1])
    y = layer_norm(y + b_dw, g_ln, b_ln)
    return jax.nn.silu(y)


def merge_groups(attn, conv, p):
    cat = jnp.concatenate([rms_norm(attn, p['g_attn_out']), rms_norm(conv, p['g_conv_out'])], axis=-1)
    return cat @ p['w_out']


def prompt_mixer(h, p):
    B, T, _ = h.shape
    q, k, v, u = project(h, p['w_in'])
    nc = T // CHUNK
    keep = min(BAND_PAST, T)
    pad = ((0, 0), (N_PAST_CHUNKS, 0), (0, 0), (0, 0), (0, 0))
    kp = jnp.pad(k.reshape(B, nc, CHUNK, N_HEADS, HEAD_DIM), pad)
    vp = jnp.pad(v.reshape(B, nc, CHUNK, N_HEADS, HEAD_DIM), pad)
    idx = jnp.arange(nc)[:, None] + jnp.arange(N_PAST_CHUNKS + 1)[None, :]
    kb = kp[:, idx].reshape(B, nc, BAND, N_HEADS, HEAD_DIM)
    vb = vp[:, idx].reshape(B, nc, BAND, N_HEADS, HEAD_DIM)
    valid = jnp.repeat(idx >= N_PAST_CHUNKS, CHUNK, axis=1)[None, :, None, None, :]
    rel = (BAND_PAST + jnp.arange(CHUNK))[:, None] - jnp.arange(BAND)[None, :]
    qc = q.reshape(B, nc, CHUNK, N_HEADS, HEAD_DIM)
    attn = band_attention(qc, kb, vb, rel_bias_table(p['rel_bias'], rel), valid).reshape(B, T, D_ATTN)
    u_pad = jnp.pad(u, ((0, 0), (CONV_STATE, 0), (0, 0)))
    conv = conv_branch(u_pad, p['w_dw'], p['b_dw'], p['g_conv_ln'], p['b_conv_ln'])
    out = merge_groups(attn, conv, p)
    return out, (u[:, T - CONV_STATE:], k[:, T - keep:], v[:, T - keep:])


def sample_mixer(h, conv_prev, k_prev, v_prev, p):
    B, S, _ = h.shape
    q, k, v, u = project(h, p['w_in'])
    keep = k_prev.shape[1]
    k_all = jnp.concatenate([k_prev, k], axis=1)[:, None]
    v_all = jnp.concatenate([v_prev, v], axis=1)[:, None]
    rel = (keep + jnp.arange(S))[:, None] - jnp.arange(keep + S)[None, :]
    attn = band_attention(q[:, None], k_all, v_all, rel_bias_table(p['rel_bias'], rel), None).reshape(B, S, D_ATTN)
    u_pad = jnp.concatenate([conv_prev, u], axis=1)
    conv = conv_branch(u_pad, p['w_dw'], p['b_dw'], p['g_conv_ln'], p['b_conv_ln'])
    out = merge_groups(attn, conv, p)
    return out, (u_pad[:, S:], k, v)


def swiglu(h, wg, wu, wd):
    return (jax.nn.silu(h @ wg) * (h @ wu)) @ wd


def trunk_layer(x, c, p, token_mixer):
    sh_m, sc_m, gt_m, sh_f, sc_f, gt_f = modulation(c, p['w_ada'], p['b_ada'])
    h = rms_norm(x, p['g_mix_pre']) * (1 + sc_m) + sh_m
    mixed, new_state = token_mixer(h)
    x = x + gt_m * rms_norm(mixed, p['g_mix_post'])
    h = rms_norm(x, p['g_ffn_pre']) * (1 + sc_f) + sh_f
    x = x + gt_f * rms_norm(swiglu(h, p['w_ffn_gate'], p['w_ffn_up'], p['w_ffn_down']), p['g_ffn_post'])
    return x, new_state


def setup_inputs(seed: int = 0) -> dict:
    key = jax.random.key(seed)
    ks = jax.random.split(key, 32)
    f32 = jnp.float32
    nrm = lambda k, shape, s: (jax.random.normal(k, shape, f32) * s).astype(f32)
    keep = min(BAND_PAST, PAST_LEN)
    return {
        'x_prompt': nrm(ks[0], (BATCH, SEQ, D_MODEL), 1.0),
        'x_sample': nrm(ks[1], (DEC_BATCH, DEC_SEQ, D_MODEL), 1.0),
        'cache_k': nrm(ks[2], (DEPTH, DEC_BATCH, keep, N_HEADS, HEAD_DIM), 1.0),
        'cache_v': nrm(ks[3], (DEPTH, DEC_BATCH, keep, N_HEADS, HEAD_DIM), 1.0),
        'state_conv': nrm(ks[4], (DEPTH, DEC_BATCH, CONV_STATE, D_CONV), 0.5),
        'c_prompt': nrm(ks[5], (BATCH, D_MODEL), 1.0),
        'c_sample': nrm(ks[6], (DEC_BATCH, D_MODEL), 1.0),
        'w_ada': nrm(ks[7], (DEPTH, D_MODEL, N_MOD * D_MODEL), 0.5 * D_MODEL ** -0.5),
        'b_ada': nrm(ks[8], (DEPTH, N_MOD * D_MODEL), 0.01),
        'g_mix_pre': 1.0 + nrm(ks[9], (DEPTH, D_MODEL), 0.01),
        'g_mix_post': 1.0 + nrm(ks[10], (DEPTH, D_MODEL), 0.01),
        'w_in': nrm(ks[11], (DEPTH, D_MODEL, D_IN), D_MODEL ** -0.5),
        'rel_bias': nrm(ks[12], (DEPTH, N_HEADS, N_REL), 0.5),
        'w_dw': nrm(ks[13], (DEPTH, CONV_WIDTH, D_CONV), CONV_WIDTH ** -0.5),
        'b_dw': nrm(ks[14], (DEPTH, D_CONV), 0.01),
        'g_conv_ln': 1.0 + nrm(ks[15], (DEPTH, D_CONV), 0.01),
        'b_conv_ln': nrm(ks[16], (DEPTH, D_CONV), 0.01),
        'g_attn_out': 1.0 + nrm(ks[17], (DEPTH, D_ATTN), 0.01),
        'g_conv_out': 1.0 + nrm(ks[18], (DEPTH, D_CONV), 0.01),
        'w_out': nrm(ks[19], (DEPTH, D_MIX, D_MODEL), D_MIX ** -0.5),
        'g_ffn_pre': 1.0 + nrm(ks[20], (DEPTH, D_MODEL), 0.01),
        'g_ffn_post': 1.0 + nrm(ks[21], (DEPTH, D_MODEL), 0.01),
        'w_ffn_gate': nrm(ks[22], (DEPTH, D_MODEL, D_FF), D_MODEL ** -0.5),
        'w_ffn_up': nrm(ks[23], (DEPTH, D_MODEL, D_FF), D_MODEL ** -0.5),
        'w_ffn_down': nrm(ks[24], (DEPTH, D_FF, D_MODEL), D_FF ** -0.5),
    }


def reference(x_prompt, x_sample, cache_k, cache_v, state_conv, c_prompt, c_sample,
              w_ada, b_ada, g_mix_pre, g_mix_post, w_in, rel_bias, w_dw, b_dw,
              g_conv_ln, b_conv_ln, g_attn_out, g_conv_out, w_out,
              g_ffn_pre, g_ffn_post, w_ffn_gate, w_ffn_up, w_ffn_down):
    yp, ys = x_prompt, x_sample
    conv_p, k_p, v_p, conv_s, k_s, v_s = [], [], [], [], [], []
    for l in range(DEPTH):
        p = {
            'w_ada': w_ada[l], 'b_ada': b_ada[l], 'g_mix_pre': g_mix_pre[l], 'g_mix_post': g_mix_post[l],
            'w_in': w_in[l], 'rel_bias': rel_bias[l], 'w_dw': w_dw[l], 'b_dw': b_dw[l],
            'g_conv_ln': g_conv_ln[l], 'b_conv_ln': b_conv_ln[l], 'g_attn_out': g_attn_out[l],
            'g_conv_out': g_conv_out[l], 'w_out': w_out[l], 'g_ffn_pre': g_ffn_pre[l],
            'g_ffn_post': g_ffn_post[l], 'w_ffn_gate': w_ffn_gate[l], 'w_ffn_up': w_ffn_up[l],
            'w_ffn_down': w_ffn_down[l],
        }
        yp, (cp, kp_new, vp_new) = trunk_layer(yp, c_prompt, p, lambda h: prompt_mixer(h, p))
        ys, (cs, ks_new, vs_new) = trunk_layer(
            ys, c_sample, p, lambda h: sample_mixer(h, state_conv[l], cache_k[l], cache_v[l], p))
        conv_p.append(cp); k_p.append(kp_new); v_p.append(vp_new)
        conv_s.append(cs); k_s.append(ks_new); v_s.append(vs_new)
    new_state_conv_prompt = jnp.stack(conv_p)
    new_cache_k_prompt = jnp.stack(k_p)
    new_cache_v_prompt = jnp.stack(v_p)
    new_state_conv_sample = jnp.stack(conv_s)
    new_cache_k_sample = jnp.stack(k_s)
    new_cache_v_sample = jnp.stack(v_s)
    return (yp, ys, new_state_conv_prompt, new_cache_k_prompt, new_cache_v_prompt,
            new_state_conv_sample, new_cache_k_sample, new_cache_v_sample)
```

```python
import functools

import numpy as np
import jax
import jax.numpy as jnp
from jax import lax
from jax.experimental import pallas as pl
from jax.experimental.pallas import tpu as pltpu

F32 = jnp.float32
BF16 = jnp.bfloat16

EPS = 1e-6
NEG_INF = -1e30
CHUNK = 64
N_PAST_CHUNKS = 8
Q_CHUNKS = 4
BAND_CHUNKS = Q_CHUNKS + N_PAST_CHUNKS
N_MOD = 6
MOD_SH_M, MOD_SC_M, MOD_GT_M, MOD_SH_F, MOD_SC_F, MOD_GT_F = range(N_MOD)

V7X_VMEM_BYTES = 64 * 1024 * 1024
V7X_VMEM_REQUEST_CAP = 60 * 1024 * 1024
V7X_SPILL_ALLOWANCE = 4 * 1024 * 1024
V7X_LANES = 128
V7X_BF16_SUBLANES = 16

STRIP = V7X_BF16_SUBLANES
HALO = 32

TM = 1024
TN_PROJ = 1024
TN_GLU = 512
TK_OUT = 512
TF_FFN = 256
TM_EW = 512
TN_ADA = 512
TT_CONV_PROMPT = 128
MOD_ROWS = 48
PROMPT_MOD_ROW = 32


def _sigmoid(x):
    return 1.0 / (1.0 + jnp.exp(-x))


def _rms(x, g):
    return x * lax.rsqrt(jnp.mean(x * x, axis=-1, keepdims=True) + EPS) * g


def _nbytes(shape, dtype):
    return int(np.prod(shape)) * jnp.dtype(dtype).itemsize


class _Specs:
    def __init__(self):
        self.bytes = 0

    def block(self, shape, index_map, dtype, buffers=2):
        real = tuple(s for s in shape if s is not None)
        self.bytes += buffers * _nbytes(real, dtype)
        if buffers == 1:
            return pl.BlockSpec(shape, index_map, pipeline_mode=pl.Buffered(1))
        return pl.BlockSpec(shape, index_map)

    def scratch(self, shape, dtype):
        self.bytes += _nbytes(shape, dtype)
        return pltpu.VMEM(shape, dtype)

    def params(self, semantics, temp_bytes=0):
        limit = min(self.bytes + temp_bytes + V7X_SPILL_ALLOWANCE, V7X_VMEM_REQUEST_CAP)
        return pltpu.CompilerParams(dimension_semantics=semantics, vmem_limit_bytes=limit)


def _ada_kernel(c_ref, w_ref, b_ref, o_ref):
    c = c_ref[...]
    a = (c * _sigmoid(c)).astype(BF16)
    w = w_ref[...].astype(BF16)
    o_ref[...] = jnp.dot(a, w, preferred_element_type=F32) + b_ref[...]


def _ada(c_all, w_ada, b_ada):
    depth, d, n = w_ada.shape
    rows = c_all.shape[0]
    sp = _Specs()
    in_specs = [
        sp.block((rows, d), lambda l, j: (0, 0), F32),
        sp.block((None, d, TN_ADA), lambda l, j: (l, 0, j), F32),
        sp.block((None, 1, TN_ADA), lambda l, j: (l, 0, j), F32),
    ]
    out_spec = sp.block((None, rows, TN_ADA), lambda l, j: (l, 0, j), F32)
    return pl.pallas_call(
        _ada_kernel,
        grid=(depth, n // TN_ADA),
        in_specs=in_specs,
        out_specs=out_spec,
        out_shape=jax.ShapeDtypeStruct((depth, rows, n), F32),
        compiler_params=sp.params(("parallel", "parallel"), temp_bytes=_nbytes((d, TN_ADA), BF16)),
        name="ada_modulation",
    )(c_all, w_ada, b_ada.reshape(depth, 1, n))


def _mod_block(sp, layer, which, tm, d, per_stream, grid_rank):
    if per_stream:
        rows = tm // CHUNK
        assert rows % 8 == 0
        if grid_rank == 1:
            imap = lambda i: (layer, i, which)
        else:
            imap = lambda i, j: (layer, i, which)
    else:
        rows = 8
        blk = PROMPT_MOD_ROW // 8
        if grid_rank == 1:
            imap = lambda i: (layer, blk, which)
        else:
            imap = lambda i, j: (layer, blk, which)
    return sp.block((None, rows, d), imap, F32)


def _mod_row(ref, strip_idx, per_stream):
    row = strip_idx // (CHUNK // STRIP) if per_stream else 0
    return ref[pl.ds(row, 1), :]


def _vec_block(sp, layer, d, grid_rank):
    if grid_rank == 1:
        imap = lambda i: (layer, 0, 0)
    else:
        imap = lambda i, j: (layer, 0, 0)
    return sp.block((None, 1, d), imap, F32)


def _prenorm_kernel(x_ref, g_ref, sc_ref, sh_ref, h_ref, *, tm, per_stream):
    def strip(j, carry):
        r = pl.ds(pl.multiple_of(j * STRIP, STRIP), STRIP)
        y = _rms(x_ref[r, :], g_ref[...])
        h = y * (1.0 + _mod_row(sc_ref, j, per_stream)) + _mod_row(sh_ref, j, per_stream)
        h_ref[r, :] = h.astype(BF16)
        return carry
    lax.fori_loop(0, tm // STRIP, strip, 0)


def _prenorm(x, g, mod, layer, per_stream):
    m, d = x.shape
    tm = TM_EW
    sp = _Specs()
    in_specs = [
        sp.block((tm, d), lambda i: (i, 0), F32),
        _vec_block(sp, layer, d, 1),
        _mod_block(sp, layer, MOD_SC_M, tm, d, per_stream, 1),
        _mod_block(sp, layer, MOD_SH_M, tm, d, per_stream, 1),
    ]
    out_spec = sp.block((tm, d), lambda i: (i, 0), BF16)
    return pl.pallas_call(
        functools.partial(_prenorm_kernel, tm=tm, per_stream=per_stream),
        grid=(m // tm,),
        in_specs=in_specs,
        out_specs=out_spec,
        out_shape=jax.ShapeDtypeStruct((m, d), BF16),
        compiler_params=sp.params(("parallel",)),
        name="prenorm",
    )(x, g, mod, mod)


def _proj_kernel(h_ref, w_ref, *out_refs, emit_bf16, emit_f32):
    acc = jnp.dot(h_ref[...], w_ref[...], preferred_element_type=F32)
    outs = list(out_refs)
    if emit_bf16:
        outs.pop(0)[...] = acc.astype(BF16)
    if emit_f32:
        outs.pop(0)[...] = acc


def _proj(h, w, layer, *, row0, m, col0, n, emit_bf16, emit_f32, tm):
    d = h.shape[1]
    tn = TN_PROJ
    assert row0 % tm == 0 and m % tm == 0 and col0 % tn == 0 and n % tn == 0
    rb, cb = row0 // tm, col0 // tn
    sp = _Specs()
    in_specs = [
        sp.block((tm, d), lambda i, j: (i + rb, 0), BF16),
        sp.block((None, d, tn), lambda i, j: (layer, 0, j + cb), BF16),
    ]
    out_specs, out_shape = [], []
    if emit_bf16:
        out_specs.append(sp.block((tm, tn), lambda i, j: (i, j), BF16))
        out_shape.append(jax.ShapeDtypeStruct((m, n), BF16))
    if emit_f32:
        out_specs.append(sp.block((tm, tn), lambda i, j: (i, j), F32))
        out_shape.append(jax.ShapeDtypeStruct((m, n), F32))
    return pl.pallas_call(
        functools.partial(_proj_kernel, emit_bf16=emit_bf16, emit_f32=emit_f32),
        grid=(m // tm, n // tn),
        in_specs=in_specs,
        out_specs=out_specs,
        out_shape=out_shape,
        compiler_params=sp.params(("parallel", "parallel"), temp_bytes=_nbytes((tm, tn), F32)),
        name="in_proj",
    )(h, w)


def _glu_kernel(h_ref, wa_ref, wb_ref, u_ref):
    h = h_ref[...]
    a = jnp.dot(h, wa_ref[...], preferred_element_type=F32)
    b = jnp.dot(h, wb_ref[...], preferred_element_type=F32)
    u_ref[...] = a * _sigmoid(b)


def _glu(h, w, layer, *, col_a, col_b, n, tm):
    m, d = h.shape
    tn = TN_GLU
    ca, cb = col_a // tn, col_b // tn
    sp = _Specs()
    in_specs = [
        sp.block((tm, d), lambda i, j: (i, 0), BF16),
        sp.block((None, d, tn), lambda i, j: (layer, 0, j + ca), BF16),
        sp.block((None, d, tn), lambda i, j: (layer, 0, j + cb), BF16),
    ]
    out_spec = sp.block((tm, tn), lambda i, j: (i, j), F32)
    return pl.pallas_call(
        _glu_kernel,
        grid=(m // tm, n // tn),
        in_specs=in_specs,
        out_specs=out_spec,
        out_shape=jax.ShapeDtypeStruct((m, n), F32),
        compiler_params=sp.params(("parallel", "parallel"), temp_bytes=3 * _nbytes((tm, tn), F32)),
        name="in_proj_glu",
    )(h, w, w)


def _softmax_pv(parts):
    m = None
    for s, _ in parts:
        mi = jnp.max(s, axis=-1, keepdims=True)
        m = mi if m is None else jnp.maximum(m, mi)
    l, o = None, None
    for s, v in parts:
        p = jnp.exp(s - m)
        li = jnp.sum(p, axis=-1, keepdims=True)
        oi = jnp.dot(p.astype(BF16), v, preferred_element_type=F32)
        l = li if l is None else l + li
        o = oi if o is None else o + oi
    return o / l


def _qk(q, k):
    return lax.dot_general(q, k, (((1,), (1,)), ((), ())), preferred_element_type=F32)


def _attn_prompt_kernel(q_ref, k_ref, v_ref, b_ref, o_ref, *, scale):
    g = pl.program_id(1)
    q = q_ref[...]
    rows = Q_CHUNKS * CHUNK

    def run(k0, n_keys, b_off):
        k = k_ref[pl.ds(k0, n_keys), :]
        v = v_ref[pl.ds(k0, n_keys), :]
        s = _qk(q, k) * scale + b_ref[:, b_off:b_off + n_keys]
        o_ref[...] = _softmax_pv([(s, v)])

    n_lead = N_PAST_CHUNKS // Q_CHUNKS
    for lead in range(n_lead):
        keys = (lead + 1) * rows
        pl.when(g == lead)(functools.partial(run, 0, keys, BAND_CHUNKS * CHUNK - keys))

    @pl.when(g >= n_lead)
    def _():
        run(pl.multiple_of((g - n_lead) * rows, rows), BAND_CHUNKS * CHUNK, 0)


def _attn_prompt(qkv, bias, layer, n_heads, hd):
    t = qkv.shape[0]
    rows = Q_CHUNKS * CHUNK
    keys = BAND_CHUNKS * CHUNK
    sp = _Specs()
    in_specs = [
        sp.block((rows, hd), lambda h, g: (g, h), BF16),
        sp.block((t, hd), lambda h, g: (0, n_heads + h), BF16),
        sp.block((t, hd), lambda h, g: (0, 2 * n_heads + h), BF16),
        sp.block((None, None, rows, keys), lambda h, g: (layer, h, 0, 0), F32),
    ]
    out_spec = sp.block((rows, hd), lambda h, g: (g, h), F32)
    return pl.pallas_call(
        functools.partial(_attn_prompt_kernel, scale=hd ** -0.5),
        grid=(n_heads, t // rows),
        in_specs=in_specs,
        out_specs=out_spec,
        out_shape=jax.ShapeDtypeStruct((t, n_heads * hd), F32),
        compiler_params=sp.params(("parallel", "parallel"), temp_bytes=6 * _nbytes((rows, keys), F32)),
        name="attn_prompt",
    )(qkv, qkv, qkv, bias)


def _attn_sample_kernel(q_ref, kn_ref, vn_ref, kc_ref, vc_ref, bc_ref, bn_ref, o_ref, *, scale, n_heads, hd):
    for h in range(n_heads):
        c = slice(h * hd, (h + 1) * hd)
        q = q_ref[:, c]
        s_cache = _qk(q, kc_ref[:, c].astype(BF16)) * scale + bc_ref[h]
        s_new = _qk(q, kn_ref[:, c]) * scale + bn_ref[h]
        o_ref[:, c] = _softmax_pv([(s_cache, vc_ref[:, c].astype(BF16)), (s_new, vn_ref[:, c])])


def _attn_sample(q, kv, cache_k, cache_v, bias_cache, bias_new, layer, n_heads, hd):
    m, da = q.shape
    n_streams = m // CHUNK
    keep = cache_k.shape[2]
    sp = _Specs()
    in_specs = [
        sp.block((CHUNK, da), lambda b: (b, 0), BF16),
        sp.block((CHUNK, da), lambda b: (b, 0), BF16),
        sp.block((CHUNK, da), lambda b: (b, 1), BF16),
        sp.block((None, None, keep, da), lambda b: (layer, b, 0, 0), F32),
        sp.block((None, None, keep, da), lambda b: (layer, b, 0, 0), F32),
        sp.block((None, n_heads, CHUNK, keep), lambda b: (layer, 0, 0, 0), F32),
        sp.block((None, n_heads, CHUNK, CHUNK), lambda b: (layer, 0, 0, 0), F32),
    ]
    out_spec = sp.block((CHUNK, da), lambda b: (b, 0), F32)
    return pl.pallas_call(
        functools.partial(_attn_sample_kernel, scale=hd ** -0.5, n_heads=n_heads, hd=hd),
        grid=(n_streams,),
        in_specs=in_specs,
        out_specs=out_spec,
        out_shape=jax.ShapeDtypeStruct((m, da), F32),
        compiler_params=sp.params(("parallel",), temp_bytes=4 * 1024 * 1024),
        name="attn_sample",
    )(q, kv, kv, cache_k, cache_v, bias_cache, bias_new)


def _bias_tables(rel_bias):
    depth, n_heads, n_rel = rel_bias.shape
    max_rel = (n_rel - 1) // 2
    rows, cols = Q_CHUNKS * CHUNK, BAND_CHUNKS * CHUNK
    p = rows + cols - 1
    rel = np.arange(p) - (cols - 1) + N_PAST_CHUNKS * CHUNK
    diag = rel_bias[:, :, np.clip(rel, -max_rel, max_rel) + max_rel]
    a = jnp.tile(diag, (1, 1, rows + 1))[:, :, :rows * (p + 1)].reshape(depth, n_heads, rows, p + 1)
    full = a[..., :cols][..., ::-1]
    qi = np.arange(rows)[:, None] // CHUNK
    kj = np.arange(cols)[None, :] // CHUNK
    valid = (kj >= qi) & (kj <= qi + N_PAST_CHUNKS)
    band = jnp.where(valid, full, NEG_INF)
    keep = N_PAST_CHUNKS * CHUNK
    return band, full[:, :, :CHUNK, :keep], full[:, :, :CHUNK, keep:keep + CHUNK]


def _conv_cat_kernel(u_ref, halo_ref, attn_ref, wdw_ref, bdw_ref, gln_ref, bln_ref, gao_ref, gco_ref,
                     cat_ref, xpad_ref, y_ref, *, tt, conv_width):
    da = attn_ref.shape[1]
    dc = u_ref.shape[1]
    xpad_ref[0:HALO, :] = halo_ref[...]
    xpad_ref[HALO:HALO + tt, :] = u_ref[...]
    first = HALO - (conv_width - 1)
    rc_rows = 64
    for rc in range(tt // rc_rows):
        for cb in range(dc // V7X_LANES):
            c = slice(cb * V7X_LANES, (cb + 1) * V7X_LANES)
            acc = None
            for w in range(conv_width):
                r0 = rc * rc_rows + first + w
                term = xpad_ref[r0:r0 + rc_rows, c] * wdw_ref[w:w + 1, c]
                acc = term if acc is None else acc + term
            y_ref[rc * rc_rows:(rc + 1) * rc_rows, c] = acc + bdw_ref[:, c]

    def strip(j, carry):
        r = pl.ds(pl.multiple_of(j * STRIP, STRIP), STRIP)
        y = y_ref[r, :]
        mu = jnp.mean(y, axis=-1, keepdims=True)
        yc = y - mu
        yn = yc * lax.rsqrt(jnp.mean(yc * yc, axis=-1, keepdims=True) + EPS) * gln_ref[...] + bln_ref[...]
        z = yn * _sigmoid(yn)
        cat_ref[r, da:da + dc] = _rms(z, gco_ref[...]).astype(BF16)
        cat_ref[r, 0:da] = _rms(attn_ref[r, :], gao_ref[...]).astype(BF16)
        return carry
    lax.fori_loop(0, tt // STRIP, strip, 0)


def _conv_cat(u, halo, attn, w_dw, b_dw, g_ln, b_ln, g_ao, g_co, layer, tt):
    m, dc = u.shape
    da = attn.shape[1]
    conv_width = w_dw.shape[1]
    assert conv_width - 1 <= HALO and halo.shape == (m // tt, HALO, dc)
    sp = _Specs()
    vec = lambda width: sp.block((None, 1, width), lambda i: (layer, 0, 0), F32)
    in_specs = [
        sp.block((tt, dc), lambda i: (i, 0), F32),
        sp.block((None, HALO, dc), lambda i: (i, 0, 0), F32),
        sp.block((tt, da), lambda i: (i, 0), F32),
        sp.block((None, conv_width, dc), lambda i: (layer, 0, 0), F32),
        vec(dc), vec(dc), vec(dc), vec(da), vec(dc),
    ]
    out_spec = sp.block((tt, da + dc), lambda i: (i, 0), BF16)
    scratch = [sp.scratch((HALO + tt, dc), F32), sp.scratch((tt, dc), F32)]
    return pl.pallas_call(
        functools.partial(_conv_cat_kernel, tt=tt, conv_width=conv_width),
        grid=(m // tt,),
        in_specs=in_specs,
        out_specs=out_spec,
        out_shape=jax.ShapeDtypeStruct((m, da + dc), BF16),
        scratch_shapes=scratch,
        compiler_params=sp.params(("parallel",), temp_bytes=4 * _nbytes((tt, dc), F32)),
        name="conv_cat",
    )(u, halo, attn, w_dw, b_dw, g_ln, b_ln, g_ao, g_co)


def _outproj_kernel(cat_ref, w_ref, x_ref, gpost_ref, gt_ref, gpre_ref, sc_ref, sh_ref, x1_ref, h2_ref,
                    *, tm, nk, per_stream):
    k = pl.program_id(1)

    @pl.when(k == 0)
    def _():
        x1_ref[...] = jnp.dot(cat_ref[...], w_ref[...], preferred_element_type=F32)

    @pl.when(k > 0)
    def _():
        x1_ref[...] += jnp.dot(cat_ref[...], w_ref[...], preferred_element_type=F32)

    @pl.when(k == nk - 1)
    def _():
        def strip(j, carry):
            r = pl.ds(pl.multiple_of(j * STRIP, STRIP), STRIP)
            x1 = x_ref[r, :] + _mod_row(gt_ref, j, per_stream) * _rms(x1_ref[r, :], gpost_ref[...])
            x1_ref[r, :] = x1
            h2 = _rms(x1, gpre_ref[...]) * (1.0 + _mod_row(sc_ref, j, per_stream)) + _mod_row(sh_ref, j, per_stream)
            h2_ref[r, :] = h2.astype(BF16)
            return carry
        lax.fori_loop(0, tm // STRIP, strip, 0)


def _outproj(cat, w_out, x, g_post, g_pre, mod, layer, per_stream, tm):
    m, dmix = cat.shape
    d = w_out.shape[2]
    tk = TK_OUT
    nk = dmix // tk
    sp = _Specs()
    in_specs = [
        sp.block((tm, tk), lambda i, k: (i, k), BF16),
        sp.block((None, tk, d), lambda i, k: (layer, k, 0), BF16),
        sp.block((tm, d), lambda i, k: (i, 0), F32, buffers=1),
        _vec_block(sp, layer, d, 2),
        _mod_block(sp, layer, MOD_GT_M, tm, d, per_stream, 2),
        _vec_block(sp, layer, d, 2),
        _mod_block(sp, layer, MOD_SC_F, tm, d, per_stream, 2),
        _mod_block(sp, layer, MOD_SH_F, tm, d, per_stream, 2),
    ]
    out_specs = [
        sp.block((tm, d), lambda i, k: (i, 0), F32, buffers=1),
        sp.block((tm, d), lambda i, k: (i, 0), BF16, buffers=1),
    ]
    return pl.pallas_call(
        functools.partial(_outproj_kernel, tm=tm, nk=nk, per_stream=per_stream),
        grid=(m // tm, nk),
        in_specs=in_specs,
        out_specs=out_specs,
        out_shape=[jax.ShapeDtypeStruct((m, d), F32), jax.ShapeDtypeStruct((m, d), BF16)],
        compiler_params=sp.params(("parallel", "arbitrary")),
        name="out_proj",
    )(cat, w_out, x, g_post, mod, g_pre, mod, mod)


def _ffn_kernel(h_ref, wg_ref, wu_ref, wd_ref, x1_ref, gpost_ref, gt_ref, x2_ref, a_ref, *, tm, nf, per_stream):
    f = pl.program_id(1)
    h = h_ref[...]
    g = jnp.dot(h, wg_ref[...], preferred_element_type=F32)
    u = jnp.dot(h, wu_ref[...], preferred_element_type=F32)
    a_ref[...] = (g * _sigmoid(g) * u).astype(BF16)

    @pl.when(f == 0)
    def _():
        x2_ref[...] = jnp.dot(a_ref[...], wd_ref[...], preferred_element_type=F32)

    @pl.when(f > 0)
    def _():
        x2_ref[...] += jnp.dot(a_ref[...], wd_ref[...], preferred_element_type=F32)

    @pl.when(f == nf - 1)
    def _():
        def strip(j, carry):
            r = pl.ds(pl.multiple_of(j * STRIP, STRIP), STRIP)
            x2_ref[r, :] = x1_ref[r, :] + _mod_row(gt_ref, j, per_stream) * _rms(x2_ref[r, :], gpost_ref[...])
            return carry
        lax.fori_loop(0, tm // STRIP, strip, 0)


def _ffn(h2, wg, wu, wd, x1, g_post, mod, layer, per_stream, tm):
    m, d = h2.shape
    dff = wg.shape[2]
    tf = TF_FFN
    nf = dff // tf
    assert dff % tf == 0
    sp = _Specs()
    in_specs = [
        sp.block((tm, d), lambda i, f: (i, 0), BF16, buffers=1),
        sp.block((None, d, tf), lambda i, f: (layer, 0, f), BF16),
        sp.block((None, d, tf), lambda i, f: (layer, 0, f), BF16),
        sp.block((None, tf, d), lambda i, f: (layer, f, 0), BF16),
        sp.block((tm, d), lambda i, f: (i, 0), F32, buffers=1),
        _vec_block(sp, layer, d, 2),
        _mod_block(sp, layer, MOD_GT_F, tm, d, per_stream, 2),
    ]
    out_spec = sp.block((tm, d), lambda i, f: (i, 0), F32, buffers=1)
    return pl.pallas_call(
        functools.partial(_ffn_kernel, tm=tm, nf=nf, per_stream=per_stream),
        grid=(m // tm, nf),
        in_specs=in_specs,
        out_specs=out_spec,
        out_shape=jax.ShapeDtypeStruct((m, d), F32),
        scratch_shapes=[sp.scratch((tm, tf), BF16)],
        compiler_params=sp.params(("parallel", "arbitrary"), temp_bytes=3 * _nbytes((tm, tf), F32)),
        name="ffn",
    )(h2, wg, wu, wd, x1, g_post, mod)


def kernel(x_prompt, x_sample, cache_k, cache_v, state_conv, c_prompt, c_sample, w_ada, b_ada, g_mix_pre,
           g_mix_post, w_in, rel_bias, w_dw, b_dw, g_conv_ln, b_conv_ln, g_attn_out, g_conv_out, w_out,
           g_ffn_pre, g_ffn_post, w_ffn_gate, w_ffn_up, w_ffn_down):
    batch, seq, d = x_prompt.shape
    dec_batch, dec_seq, _ = x_sample.shape
    depth, _, keep, n_heads, hd = cache_k.shape
    da = n_heads * hd
    dc = state_conv.shape[3]
    conv_state = state_conv.shape[2]
    assert batch == 1 and dec_seq == CHUNK and keep == N_PAST_CHUNKS * CHUNK and dec_batch == PROMPT_MOD_ROW
    assert seq % (Q_CHUNKS * CHUNK) == 0 and seq >= keep
    m_s = dec_batch * dec_seq

    vec = lambda a: a.reshape(depth, 1, a.shape[-1])
    g_mix_pre, g_mix_post, g_ffn_pre, g_ffn_post = map(vec, (g_mix_pre, g_mix_post, g_ffn_pre, g_ffn_post))
    b_dw, g_conv_ln, b_conv_ln, g_attn_out, g_conv_out = map(vec, (b_dw, g_conv_ln, b_conv_ln, g_attn_out, g_conv_out))

    c_all = jnp.concatenate(
        [c_sample, c_prompt, jnp.zeros((MOD_ROWS - dec_batch - batch, d), F32)], axis=0)
    mod = _ada(c_all, w_ada, b_ada)

    w_in_b, w_out_b = w_in.astype(BF16), w_out.astype(BF16)
    wg_b, wu_b, wd_b = w_ffn_gate.astype(BF16), w_ffn_up.astype(BF16), w_ffn_down.astype(BF16)
    band_bias, cache_bias, new_bias = _bias_tables(rel_bias)
    cache_k2 = cache_k.reshape(depth, dec_batch, keep, da)
    cache_v2 = cache_v.reshape(depth, dec_batch, keep, da)

    xp = x_prompt.reshape(seq, d)
    xs = x_sample.reshape(m_s, d)
    outs = {k: [] for k in ("conv_p", "k_p", "v_p", "conv_s", "k_s", "v_s")}
    tt_p = TT_CONV_PROMPT
    for l in range(depth):
        h = _prenorm(xp, g_mix_pre, mod, l, per_stream=False)
        (qkv,) = _proj(h, w_in_b, l, row0=0, m=seq, col0=0, n=3 * da, emit_bf16=True, emit_f32=False, tm=TM)
        (kv_tail,) = _proj(h, w_in_b, l, row0=seq - keep, m=keep, col0=da, n=2 * da,
                           emit_bf16=False, emit_f32=True, tm=keep)
        u = _glu(h, w_in_b, l, col_a=3 * da, col_b=3 * da + dc, n=dc, tm=TM)
        attn = _attn_prompt(qkv, band_bias, l, n_heads, hd)
        prev_tail = u.reshape(seq // tt_p, tt_p, dc)[:-1, tt_p - HALO:, :]
        halo = jnp.concatenate([jnp.zeros((1, HALO, dc), F32), prev_tail], axis=0)
        cat = _conv_cat(u, halo, attn, w_dw, b_dw, g_conv_ln, b_conv_ln, g_attn_out, g_conv_out, l, tt_p)
        x1, h2 = _outproj(cat, w_out_b, xp, g_mix_post, g_ffn_pre, mod, l, False, TM)
        xp = _ffn(h2, wg_b, wu_b, wd_b, x1, g_ffn_post, mod, l, False, TM)
        outs["conv_p"].append(u[seq - conv_state:].reshape(batch, conv_state, dc))
        outs["k_p"].append(kv_tail[:, :da].reshape(batch, keep, n_heads, hd))
        outs["v_p"].append(kv_tail[:, da:].reshape(batch, keep, n_heads, hd))

        h = _prenorm(xs, g_mix_pre, mod, l, per_stream=True)
        (q_s,) = _proj(h, w_in_b, l, row0=0, m=m_s, col0=0, n=da, emit_bf16=True, emit_f32=False, tm=TM)
        kv_s, kv_s32 = _proj(h, w_in_b, l, row0=0, m=m_s, col0=da, n=2 * da, emit_bf16=True, emit_f32=True, tm=TM)
        u = _glu(h, w_in_b, l, col_a=3 * da, col_b=3 * da + dc, n=dc, tm=TM)
        attn = _attn_sample(q_s, kv_s, cache_k2, cache_v2, cache_bias, new_bias, l, n_heads, hd)
        halo = jnp.pad(state_conv[l], ((0, 0), (HALO - conv_state, 0), (0, 0)))
        cat = _conv_cat(u, halo, attn, w_dw, b_dw, g_conv_ln, b_conv_ln, g_attn_out, g_conv_out, l, CHUNK)
        x1, h2 = _outproj(cat, w_out_b, xs, g_mix_post, g_ffn_pre, mod, l, True, TM)
        xs = _ffn(h2, wg_b, wu_b, wd_b, x1, g_ffn_post, mod, l, True, TM)
        outs["conv_s"].append(u.reshape(dec_batch, dec_seq, dc)[:, dec_seq - conv_state:, :])
        outs["k_s"].append(kv_s32[:, :da].reshape(dec_batch, dec_seq, n_heads, hd))
        outs["v_s"].append(kv_s32[:, da:].reshape(dec_batch, dec_seq, n_heads, hd))

    stack = lambda name: jnp.stack(outs[name])
    return (xp.reshape(batch, seq, d), xs.reshape(dec_batch, dec_seq, d),
            stack("conv_p"), stack("k_p"), stack("v_p"), stack("conv_s"), stack("k_s"), stack("v_s"))
```

```python
import functools

import numpy as np
import jax
import jax.numpy as jnp
from jax import lax
from jax.experimental import pallas as pl
from jax.experimental.pallas import tpu as pltpu

F32 = jnp.float32
BF16 = jnp.bfloat16

EPS = 1e-6
NEG_INF = -1e30
CHUNK = 64
N_PAST_CHUNKS = 8
Q_CHUNKS = 4
BAND_CHUNKS = Q_CHUNKS + N_PAST_CHUNKS
N_MOD = 6
MOD_SH_M, MOD_SC_M, MOD_GT_M, MOD_SH_F, MOD_SC_F, MOD_GT_F = range(N_MOD)

V7X_VMEM_BYTES = 64 * 1024 * 1024
V7X_VMEM_REQUEST_CAP = 60 * 1024 * 1024
V7X_SPILL_ALLOWANCE = 4 * 1024 * 1024
V7X_LANES = 128
V7X_SUBLANES = 8
V7X_BF16_SUBLANES = 16

STRIP = 2 * V7X_BF16_SUBLANES
HALO = 32
NORM_UNROLL = 1

TM = 1024
TN_PROJ = 1024
TN_GLU = 512
TM_OUT = 512
TK_OUT = 512
TF_FFN = 256
TM_EW = 512
TN_ADA = 512
TT_CONV_PROMPT = 128
ATTN_HEADS_PER_STEP = 4
MOD_ROWS = 48
PROMPT_MOD_ROW = 32


def _sigmoid(x):
    return 1.0 / (1.0 + jnp.exp(-x))


def _nbytes(shape, dtype):
    return int(np.prod(shape)) * jnp.dtype(dtype).itemsize


class _Specs:
    def __init__(self):
        self.bytes = 0

    def block(self, shape, index_map, dtype, buffers=2):
        real = tuple(s for s in shape if s is not None)
        self.bytes += buffers * _nbytes(real, dtype)
        if buffers == 1:
            return pl.BlockSpec(shape, index_map, pipeline_mode=pl.Buffered(1))
        return pl.BlockSpec(shape, index_map)

    def scratch(self, shape, dtype):
        self.bytes += _nbytes(shape, dtype)
        return pltpu.VMEM(shape, dtype)

    def params(self, semantics, temp_bytes=0):
        limit = min(self.bytes + temp_bytes + V7X_SPILL_ALLOWANCE, V7X_VMEM_REQUEST_CAP)
        return pltpu.CompilerParams(dimension_semantics=semantics, vmem_limit_bytes=limit)


def _ada_kernel(c_ref, w_ref, b_ref, gmpre_ref, gmpost_ref, gfpre_ref, gfpost_ref, o_ref, *, blocks_per_mod):
    which = pl.program_id(1) // blocks_per_mod
    c = c_ref[...]
    a = (c * _sigmoid(c)).astype(BF16)
    w = w_ref[...].astype(BF16)
    m = jnp.dot(a, w, preferred_element_type=F32) + b_ref[...]
    g_scale = jnp.where(which == MOD_SC_M, gmpre_ref[...], gfpre_ref[...])
    g_gate = jnp.where(which == MOD_GT_M, gmpost_ref[...], gfpost_ref[...])
    is_scale = jnp.logical_or(which == MOD_SC_M, which == MOD_SC_F)
    is_gate = jnp.logical_or(which == MOD_GT_M, which == MOD_GT_F)
    o_ref[...] = jnp.where(is_scale, g_scale * (1.0 + m), jnp.where(is_gate, g_gate * m, m))


def _ada(c_all, w_ada, b_ada, g_mix_pre, g_mix_post, g_ffn_pre, g_ffn_post):
    depth, d, n = w_ada.shape
    rows = c_all.shape[0]
    blocks_per_mod = d // TN_ADA
    sp = _Specs()
    gain = lambda: sp.block((None, 1, TN_ADA), lambda l, j: (l, 0, j % blocks_per_mod), F32)
    in_specs = [
        sp.block((rows, d), lambda l, j: (0, 0), F32),
        sp.block((None, d, TN_ADA), lambda l, j: (l, 0, j), F32),
        sp.block((None, 1, TN_ADA), lambda l, j: (l, 0, j), F32),
        gain(), gain(), gain(), gain(),
    ]
    out_spec = sp.block((None, rows, TN_ADA), lambda l, j: (l, 0, j), F32)
    return pl.pallas_call(
        functools.partial(_ada_kernel, blocks_per_mod=blocks_per_mod),
        grid=(depth, n // TN_ADA),
        in_specs=in_specs,
        out_specs=out_spec,
        out_shape=jax.ShapeDtypeStruct((depth, rows, n), F32),
        compiler_params=sp.params(("parallel", "parallel"), temp_bytes=_nbytes((d, TN_ADA), BF16)),
        name="ada_modulation",
    )(c_all, w_ada, b_ada.reshape(depth, 1, n), g_mix_pre, g_mix_post, g_ffn_pre, g_ffn_post)


def _mod_block(sp, layer, which, tm, d, per_stream, grid_rank):
    if per_stream:
        rows = tm // CHUNK
        assert rows % 8 == 0
        if grid_rank == 1:
            imap = lambda i: (layer, i, which)
        else:
            imap = lambda i, j: (layer, i, which)
    else:
        rows = 8
        blk = PROMPT_MOD_ROW // 8
        if grid_rank == 1:
            imap = lambda i: (layer, blk, which)
        else:
            imap = lambda i, j: (layer, blk, which)
    return sp.block((None, rows, d), imap, F32)


def _strip_loops(tm, per_stream, vec_refs, bc_ref, strip_fn):
    rows_per_vec = CHUNK if per_stream else tm
    n_inner = rows_per_vec // STRIP

    def outer(s, carry):
        for k, ref in enumerate(vec_refs):
            bc_ref[k] = jnp.broadcast_to(ref[pl.ds(s, 1), :], bc_ref.shape[1:])

        def inner(j, c2):
            strip_fn(pl.ds(pl.multiple_of(s * rows_per_vec + j * STRIP, STRIP), STRIP))
            return c2
        lax.fori_loop(0, n_inner, inner, 0, unroll=min(n_inner, NORM_UNROLL))
        return carry
    lax.fori_loop(0, tm // rows_per_vec, outer, 0)


def _split(x):
    return x.reshape(STRIP // V7X_SUBLANES, V7X_SUBLANES, x.shape[-1])


def _merge(x3):
    return x3.reshape(STRIP, x3.shape[-1])


def _inv_rms(x):
    return lax.rsqrt(jnp.mean(x * x, axis=-1, keepdims=True) + EPS)


def _prenorm_kernel(x_ref, sc_ref, sh_ref, h_ref, bc_ref, *, tm, per_stream):
    def strip(r):
        x = _split(x_ref[r, :])
        h_ref[r, :] = _merge(x * _inv_rms(x) * bc_ref[0] + bc_ref[1]).astype(BF16)
    _strip_loops(tm, per_stream, [sc_ref, sh_ref], bc_ref, strip)


def _prenorm(x, mod, layer, per_stream):
    m, d = x.shape
    tm = TM_EW
    sp = _Specs()
    in_specs = [
        sp.block((tm, d), lambda i: (i, 0), F32),
        _mod_block(sp, layer, MOD_SC_M, tm, d, per_stream, 1),
        _mod_block(sp, layer, MOD_SH_M, tm, d, per_stream, 1),
    ]
    out_spec = sp.block((tm, d), lambda i: (i, 0), BF16)
    return pl.pallas_call(
        functools.partial(_prenorm_kernel, tm=tm, per_stream=per_stream),
        grid=(m // tm,),
        in_specs=in_specs,
        out_specs=out_spec,
        out_shape=jax.ShapeDtypeStruct((m, d), BF16),
        scratch_shapes=[sp.scratch((2, V7X_SUBLANES, d), F32)],
        compiler_params=sp.params(("parallel",)),
        name="prenorm",
    )(x, mod, mod)


def _proj_kernel(h_ref, w_ref, *out_refs, emit_bf16, emit_f32):
    acc = jnp.dot(h_ref[...], w_ref[...], preferred_element_type=F32)
    outs = list(out_refs)
    if emit_bf16:
        outs.pop(0)[...] = acc.astype(BF16)
    if emit_f32:
        outs.pop(0)[...] = acc


def _proj(h, w, layer, *, row0, m, col0, n, emit_bf16, emit_f32, tm):
    d = h.shape[1]
    tn = TN_PROJ
    assert row0 % tm == 0 and m % tm == 0 and col0 % tn == 0 and n % tn == 0
    rb, cb = row0 // tm, col0 // tn
    sp = _Specs()
    in_specs = [
        sp.block((tm, d), lambda i, j: (i + rb, 0), BF16),
        sp.block((None, d, tn), lambda i, j: (layer, 0, j + cb), BF16),
    ]
    out_specs, out_shape = [], []
    if emit_bf16:
        out_specs.append(sp.block((tm, tn), lambda i, j: (i, j), BF16))
        out_shape.append(jax.ShapeDtypeStruct((m, n), BF16))
    if emit_f32:
        out_specs.append(sp.block((tm, tn), lambda i, j: (i, j), F32))
        out_shape.append(jax.ShapeDtypeStruct((m, n), F32))
    return pl.pallas_call(
        functools.partial(_proj_kernel, emit_bf16=emit_bf16, emit_f32=emit_f32),
        grid=(m // tm, n // tn),
        in_specs=in_specs,
        out_specs=out_specs,
        out_shape=out_shape,
        compiler_params=sp.params(("parallel", "parallel"), temp_bytes=_nbytes((tm, tn), F32)),
        name="in_proj",
    )(h, w)


def _glu_kernel(h_ref, wa_ref, wb_ref, u_ref):
    h = h_ref[...]
    a = jnp.dot(h, wa_ref[...], preferred_element_type=F32)
    b = jnp.dot(h, wb_ref[...], preferred_element_type=F32)
    u_ref[...] = a * _sigmoid(b)


def _glu(h, w, layer, *, col_a, col_b, n, tm):
    m, d = h.shape
    tn = TN_GLU
    ca, cb = col_a // tn, col_b // tn
    sp = _Specs()
    in_specs = [
        sp.block((tm, d), lambda i, j: (i, 0), BF16),
        sp.block((None, d, tn), lambda i, j: (layer, 0, j + ca), BF16),
        sp.block((None, d, tn), lambda i, j: (layer, 0, j + cb), BF16),
    ]
    out_spec = sp.block((tm, tn), lambda i, j: (i, j), F32)
    return pl.pallas_call(
        _glu_kernel,
        grid=(m // tm, n // tn),
        in_specs=in_specs,
        out_specs=out_spec,
        out_shape=jax.ShapeDtypeStruct((m, n), F32),
        compiler_params=sp.params(("parallel", "parallel"), temp_bytes=3 * _nbytes((tm, tn), F32)),
        name="in_proj_glu",
    )(h, w, w)


def _softmax_pv(parts):
    m = None
    for s, _ in parts:
        mi = jnp.max(s, axis=-1, keepdims=True)
        m = mi if m is None else jnp.maximum(m, mi)
    l, o = None, None
    for s, v in parts:
        p = jnp.exp(s - m)
        li = jnp.sum(p, axis=-1, keepdims=True)
        oi = jnp.dot(p.astype(BF16), v, preferred_element_type=F32)
        l = li if l is None else l + li
        o = oi if o is None else o + oi
    return o / l


def _qk(q, k):
    return lax.dot_general(q, k, (((1,), (1,)), ((), ())), preferred_element_type=F32)


def _attn_prompt_kernel(q_ref, k_ref, v_ref, b_ref, o_ref, *, scale, hd, heads):
    g = pl.program_id(1)
    rows = Q_CHUNKS * CHUNK

    def run(k0, n_keys, b_off):
        for h in range(heads):
            c = slice(h * hd, (h + 1) * hd)
            k = k_ref[pl.ds(k0, n_keys), c]
            v = v_ref[pl.ds(k0, n_keys), c]
            s = _qk(q_ref[:, c], k) * scale + b_ref[h, :, b_off:b_off + n_keys]
            o_ref[:, c] = _softmax_pv([(s, v)])

    n_lead = N_PAST_CHUNKS // Q_CHUNKS
    for lead in range(n_lead):
        keys = (lead + 1) * rows
        pl.when(g == lead)(functools.partial(run, 0, keys, BAND_CHUNKS * CHUNK - keys))

    @pl.when(g >= n_lead)
    def _():
        run(pl.multiple_of((g - n_lead) * rows, rows), BAND_CHUNKS * CHUNK, 0)


def _attn_prompt(qkv, bias, layer, n_heads, hd):
    t = qkv.shape[0]
    rows = Q_CHUNKS * CHUNK
    keys = BAND_CHUNKS * CHUNK
    heads = ATTN_HEADS_PER_STEP
    assert n_heads % heads == 0
    n_groups = n_heads // heads
    width = heads * hd
    sp = _Specs()
    in_specs = [
        sp.block((rows, width), lambda h, g: (g, h), BF16),
        sp.block((t, width), lambda h, g: (0, n_groups + h), BF16),
        sp.block((t, width), lambda h, g: (0, 2 * n_groups + h), BF16),
        sp.block((None, heads, rows, keys), lambda h, g: (layer, h, 0, 0), F32),
    ]
    out_spec = sp.block((rows, width), lambda h, g: (g, h), F32)
    return pl.pallas_call(
        functools.partial(_attn_prompt_kernel, scale=hd ** -0.5, hd=hd, heads=heads),
        grid=(n_groups, t // rows),
        in_specs=in_specs,
        out_specs=out_spec,
        out_shape=jax.ShapeDtypeStruct((t, n_heads * hd), F32),
        compiler_params=sp.params(("parallel", "parallel"), temp_bytes=6 * _nbytes((rows, keys), F32)),
        name="attn_prompt",
    )(qkv, qkv, qkv, bias)


def _attn_sample_kernel(q_ref, kn_ref, vn_ref, kc_ref, vc_ref, bc_ref, bn_ref, o_ref, *, scale, n_heads, hd):
    keep = kc_ref.shape[0] // n_heads
    for h in range(n_heads):
        c = slice(h * hd, (h + 1) * hd)
        frames = pl.ds(h, keep, stride=n_heads)
        q = q_ref[:, c]
        s_cache = _qk(q, kc_ref[frames, :].astype(BF16)) * scale + bc_ref[h]
        s_new = _qk(q, kn_ref[:, c]) * scale + bn_ref[h]
        o_ref[:, c] = _softmax_pv([(s_cache, vc_ref[frames, :].astype(BF16)), (s_new, vn_ref[:, c])])


def _attn_sample(q, kv, cache_k, cache_v, bias_cache, bias_new, layer, n_heads, hd):
    m, da = q.shape
    n_streams = m // CHUNK
    keep = cache_k.shape[2] // n_heads
    sp = _Specs()
    in_specs = [
        sp.block((CHUNK, da), lambda b: (b, 0), BF16),
        sp.block((CHUNK, da), lambda b: (b, 0), BF16),
        sp.block((CHUNK, da), lambda b: (b, 1), BF16),
        sp.block((None, None, keep * n_heads, hd), lambda b: (layer, b, 0, 0), F32),
        sp.block((None, None, keep * n_heads, hd), lambda b: (layer, b, 0, 0), F32),
        sp.block((None, n_heads, CHUNK, keep), lambda b: (layer, 0, 0, 0), F32),
        sp.block((None, n_heads, CHUNK, CHUNK), lambda b: (layer, 0, 0, 0), F32),
    ]
    out_spec = sp.block((CHUNK, da), lambda b: (b, 0), F32)
    return pl.pallas_call(
        functools.partial(_attn_sample_kernel, scale=hd ** -0.5, n_heads=n_heads, hd=hd),
        grid=(n_streams,),
        in_specs=in_specs,
        out_specs=out_spec,
        out_shape=jax.ShapeDtypeStruct((m, da), F32),
        compiler_params=sp.params(("parallel",), temp_bytes=4 * 1024 * 1024),
        name="attn_sample",
    )(q, kv, kv, cache_k, cache_v, bias_cache, bias_new)


def _bias_tables(rel_bias):
    depth, n_heads, n_rel = rel_bias.shape
    max_rel = (n_rel - 1) // 2
    rows, cols = Q_CHUNKS * CHUNK, BAND_CHUNKS * CHUNK
    p = rows + cols - 1
    x = np.arange(p)
    rel = np.where(x < cols, -x, p - x) + N_PAST_CHUNKS * CHUNK
    diag = rel_bias[:, :, np.clip(rel, -max_rel, max_rel) + max_rel]
    a = jnp.tile(diag, (1, 1, rows))[:, :, :rows * (p - 1)].reshape(depth, n_heads, rows, p - 1)
    full = a[..., :cols]
    qi = np.arange(rows)[:, None] // CHUNK
    kj = np.arange(cols)[None, :] // CHUNK
    valid = (kj >= qi) & (kj <= qi + N_PAST_CHUNKS)
    band = jnp.where(valid, full, NEG_INF)
    keep = N_PAST_CHUNKS * CHUNK
    return band, full[:, :, :CHUNK, :keep], full[:, :, :CHUNK, keep:keep + CHUNK]


def _conv_cat_kernel(u_ref, halo_ref, attn_ref, wdw_ref, bdw_ref, gln_ref, bln_ref, gao_ref, gco_ref,
                     cat_ref, xs_ref, y_ref, bc_ref, *, tt, conv_width):
    da = attn_ref.shape[1]
    dc = u_ref.shape[1]
    xs_ref[0, 0:HALO, :] = halo_ref[...]
    xs_ref[0, HALO:HALO + tt, :] = u_ref[...]
    first = HALO - (conv_width - 1)
    n_shift = HALO + tt - V7X_SUBLANES
    for r in range(1, V7X_SUBLANES):
        xs_ref[r, 0:n_shift, :] = xs_ref[0, r:r + n_shift, :]
    rc_rows = 64
    for rc in range(tt // rc_rows):
        for cb in range(dc // V7X_LANES):
            c = slice(cb * V7X_LANES, (cb + 1) * V7X_LANES)
            acc = None
            for w in range(conv_width):
                r0 = rc * rc_rows + (first + w) // V7X_SUBLANES * V7X_SUBLANES
                term = xs_ref[(first + w) % V7X_SUBLANES, r0:r0 + rc_rows, c] * wdw_ref[w:w + 1, c]
                acc = term if acc is None else acc + term
            y_ref[rc * rc_rows:(rc + 1) * rc_rows, c] = acc + bdw_ref[:, c]

    for k, ref in enumerate((gln_ref, bln_ref, gco_ref, gao_ref)):
        bc_ref[k] = jnp.broadcast_to(ref[...], bc_ref.shape[1:])

    def strip(j, carry):
        r = pl.ds(pl.multiple_of(j * STRIP, STRIP), STRIP)
        y = _split(y_ref[r, :])
        yc = y - jnp.mean(y, axis=-1, keepdims=True)
        yn = yc * _inv_rms(yc) * bc_ref[0] + bc_ref[1]
        z = yn * _sigmoid(yn)
        cat_ref[r, da:da + dc] = _merge(z * _inv_rms(z) * bc_ref[2]).astype(BF16)
        a = _split(attn_ref[r, :])
        cat_ref[r, 0:da] = _merge(a * _inv_rms(a) * bc_ref[3]).astype(BF16)
        return carry
    lax.fori_loop(0, tt // STRIP, strip, 0, unroll=NORM_UNROLL)


def _conv_cat(u, halo, attn, w_dw, b_dw, g_ln, b_ln, g_ao, g_co, layer, tt):
    m, dc = u.shape
    da = attn.shape[1]
    conv_width = w_dw.shape[1]
    assert conv_width - 1 <= HALO and halo.shape == (m // tt, HALO, dc)
    assert da == dc
    sp = _Specs()
    vec = lambda width: sp.block((None, 1, width), lambda i: (layer, 0, 0), F32)
    in_specs = [
        sp.block((tt, dc), lambda i: (i, 0), F32),
        sp.block((None, HALO, dc), lambda i: (i, 0, 0), F32),
        sp.block((tt, da), lambda i: (i, 0), F32),
        sp.block((None, conv_width, dc), lambda i: (layer, 0, 0), F32),
        vec(dc), vec(dc), vec(dc), vec(da), vec(dc),
    ]
    out_spec = sp.block((tt, da + dc), lambda i: (i, 0), BF16)
    scratch = [sp.scratch((V7X_SUBLANES, HALO + tt, dc), F32), sp.scratch((tt, dc), F32),
               sp.scratch((4, V7X_SUBLANES, dc), F32)]
    return pl.pallas_call(
        functools.partial(_conv_cat_kernel, tt=tt, conv_width=conv_width),
        grid=(m // tt,),
        in_specs=in_specs,
        out_specs=out_spec,
        out_shape=jax.ShapeDtypeStruct((m, da + dc), BF16),
        scratch_shapes=scratch,
        compiler_params=sp.params(("parallel",), temp_bytes=4 * _nbytes((tt, dc), F32)),
        name="conv_cat",
    )(u, halo, attn, w_dw, b_dw, g_ln, b_ln, g_ao, g_co)


def _outproj_kernel(cat_ref, w_ref, x_ref, gt_ref, sc_ref, sh_ref, x1_ref, h2_ref, bc_ref, *, tm, nk, per_stream):
    k = pl.program_id(1)

    @pl.when(k == 0)
    def _():
        x1_ref[...] = jnp.dot(cat_ref[...], w_ref[...], preferred_element_type=F32)

    @pl.when(k > 0)
    def _():
        x1_ref[...] += jnp.dot(cat_ref[...], w_ref[...], preferred_element_type=F32)

    @pl.when(k == nk - 1)
    def _():
        def strip(r):
            mixed = _split(x1_ref[r, :])
            x1 = _split(x_ref[r, :]) + mixed * _inv_rms(mixed) * bc_ref[0]
            x1_ref[r, :] = _merge(x1)
            h2_ref[r, :] = _merge(x1 * _inv_rms(x1) * bc_ref[1] + bc_ref[2]).astype(BF16)
        _strip_loops(tm, per_stream, [gt_ref, sc_ref, sh_ref], bc_ref, strip)


def _outproj(cat, w_out, x, mod, layer, per_stream, tm):
    m, dmix = cat.shape
    d = w_out.shape[2]
    tk = TK_OUT
    nk = dmix // tk
    sp = _Specs()
    in_specs = [
        sp.block((tm, tk), lambda i, k: (i, k), BF16),
        sp.block((None, tk, d), lambda i, k: (layer, k, 0), BF16),
        sp.block((tm, d), lambda i, k: (i, 0), F32),
        _mod_block(sp, layer, MOD_GT_M, tm, d, per_stream, 2),
        _mod_block(sp, layer, MOD_SC_F, tm, d, per_stream, 2),
        _mod_block(sp, layer, MOD_SH_F, tm, d, per_stream, 2),
    ]
    out_specs = [
        sp.block((tm, d), lambda i, k: (i, 0), F32),
        sp.block((tm, d), lambda i, k: (i, 0), BF16),
    ]
    return pl.pallas_call(
        functools.partial(_outproj_kernel, tm=tm, nk=nk, per_stream=per_stream),
        grid=(m // tm, nk),
        in_specs=in_specs,
        out_specs=out_specs,
        out_shape=[jax.ShapeDtypeStruct((m, d), F32), jax.ShapeDtypeStruct((m, d), BF16)],
        scratch_shapes=[sp.scratch((3, V7X_SUBLANES, d), F32)],
        compiler_params=sp.params(("parallel", "arbitrary")),
        name="out_proj",
    )(cat, w_out, x, mod, mod, mod)


def _ffn_kernel(h_ref, wg_ref, wu_ref, wd_ref, x1_ref, gt_ref, x2_ref, a_ref, bc_ref, *, tm, nf, per_stream):
    f = pl.program_id(1)
    h = h_ref[...]
    g = jnp.dot(h, wg_ref[...], preferred_element_type=F32)
    u = jnp.dot(h, wu_ref[...], preferred_element_type=F32)
    a_ref[...] = (g * _sigmoid(g) * u).astype(BF16)

    @pl.when(f == 0)
    def _():
        x2_ref[...] = jnp.dot(a_ref[...], wd_ref[...], preferred_element_type=F32)

    @pl.when(f > 0)
    def _():
        x2_ref[...] += jnp.dot(a_ref[...], wd_ref[...], preferred_element_type=F32)

    @pl.when(f == nf - 1)
    def _():
        def strip(r):
            y = _split(x2_ref[r, :])
            x2_ref[r, :] = _merge(_split(x1_ref[r, :]) + y * _inv_rms(y) * bc_ref[0])
        _strip_loops(tm, per_stream, [gt_ref], bc_ref, strip)


def _ffn(h2, wg, wu, wd, x1, mod, layer, per_stream, tm):
    m, d = h2.shape
    dff = wg.shape[2]
    tf = TF_FFN
    nf = dff // tf
    assert dff % tf == 0
    sp = _Specs()
    in_specs = [
        sp.block((tm, d), lambda i, f: (i, 0), BF16, buffers=1),
        sp.block((None, d, tf), lambda i, f: (layer, 0, f), BF16),
        sp.block((None, d, tf), lambda i, f: (layer, 0, f), BF16),
        sp.block((None, tf, d), lambda i, f: (layer, f, 0), BF16),
        sp.block((tm, d), lambda i, f: (i, 0), F32, buffers=1),
        _mod_block(sp, layer, MOD_GT_F, tm, d, per_stream, 2),
    ]
    out_spec = sp.block((tm, d), lambda i, f: (i, 0), F32, buffers=1)
    return pl.pallas_call(
        functools.partial(_ffn_kernel, tm=tm, nf=nf, per_stream=per_stream),
        grid=(m // tm, nf),
        in_specs=in_specs,
        out_specs=out_spec,
        out_shape=jax.ShapeDtypeStruct((m, d), F32),
        scratch_shapes=[sp.scratch((tm, tf), BF16), sp.scratch((1, V7X_SUBLANES, d), F32)],
        compiler_params=sp.params(("parallel", "arbitrary"), temp_bytes=3 * _nbytes((tm, tf), F32)),
        name="ffn",
    )(h2, wg, wu, wd, x1, mod)


def kernel(x_prompt, x_sample, cache_k, cache_v, state_conv, c_prompt, c_sample, w_ada, b_ada, g_mix_pre,
           g_mix_post, w_in, rel_bias, w_dw, b_dw, g_conv_ln, b_conv_ln, g_attn_out, g_conv_out, w_out,
           g_ffn_pre, g_ffn_post, w_ffn_gate, w_ffn_up, w_ffn_down):
    batch, seq, d = x_prompt.shape
    dec_batch, dec_seq, _ = x_sample.shape
    depth, _, keep, n_heads, hd = cache_k.shape
    da = n_heads * hd
    dc = state_conv.shape[3]
    conv_state = state_conv.shape[2]
    assert batch == 1 and dec_seq == CHUNK and keep == N_PAST_CHUNKS * CHUNK and dec_batch == PROMPT_MOD_ROW
    assert seq % (Q_CHUNKS * CHUNK) == 0 and seq >= keep
    m_s = dec_batch * dec_seq

    vec = lambda a: a.reshape(depth, 1, a.shape[-1])
    g_mix_pre, g_mix_post, g_ffn_pre, g_ffn_post = map(vec, (g_mix_pre, g_mix_post, g_ffn_pre, g_ffn_post))
    b_dw, g_conv_ln, b_conv_ln, g_attn_out, g_conv_out = map(vec, (b_dw, g_conv_ln, b_conv_ln, g_attn_out, g_conv_out))

    c_all = jnp.concatenate(
        [c_sample, c_prompt, jnp.zeros((MOD_ROWS - dec_batch - batch, d), F32)], axis=0)
    mod = _ada(c_all, w_ada, b_ada, g_mix_pre, g_mix_post, g_ffn_pre, g_ffn_post)

    w_in_b, w_out_b = w_in.astype(BF16), w_out.astype(BF16)
    wg_b, wu_b, wd_b = w_ffn_gate.astype(BF16), w_ffn_up.astype(BF16), w_ffn_down.astype(BF16)
    band_bias, cache_bias, new_bias = _bias_tables(rel_bias)
    cache_k2 = cache_k.reshape(depth, dec_batch, keep * n_heads, hd)
    cache_v2 = cache_v.reshape(depth, dec_batch, keep * n_heads, hd)

    xp = x_prompt.reshape(seq, d)
    xs = x_sample.reshape(m_s, d)
    outs = {k: [] for k in ("conv_p", "k_p", "v_p", "conv_s", "k_s", "v_s")}
    tt_p = TT_CONV_PROMPT
    for l in range(depth):
        h = _prenorm(xp, mod, l, per_stream=False)
        (qkv,) = _proj(h, w_in_b, l, row0=0, m=seq, col0=0, n=3 * da, emit_bf16=True, emit_f32=False, tm=TM)
        (kv_tail,) = _proj(h, w_in_b, l, row0=seq - keep, m=keep, col0=da, n=2 * da,
                           emit_bf16=False, emit_f32=True, tm=keep)
        u = _glu(h, w_in_b, l, col_a=3 * da, col_b=3 * da + dc, n=dc, tm=TM)
        attn = _attn_prompt(qkv, band_bias, l, n_heads, hd)
        prev_tail = u.reshape(seq // tt_p, tt_p, dc)[:-1, tt_p - HALO:, :]
        halo = jnp.concatenate([jnp.zeros((1, HALO, dc), F32), prev_tail], axis=0)
        cat = _conv_cat(u, halo, attn, w_dw, b_dw, g_conv_ln, b_conv_ln, g_attn_out, g_conv_out, l, tt_p)
        x1, h2 = _outproj(cat, w_out_b, xp, mod, l, False, TM_OUT)
        xp = _ffn(h2, wg_b, wu_b, wd_b, x1, mod, l, False, TM)
        outs["conv_p"].append(u[seq - conv_state:].reshape(batch, conv_state, dc))
        outs["k_p"].append(kv_tail[:, :da].reshape(batch, keep, n_heads, hd))
        outs["v_p"].append(kv_tail[:, da:].reshape(batch, keep, n_heads, hd))

        h = _prenorm(xs, mod, l, per_stream=True)
        (q_s,) = _proj(h, w_in_b, l, row0=0, m=m_s, col0=0, n=da, emit_bf16=True, emit_f32=False, tm=TM)
        kv_s, kv_s32 = _proj(h, w_in_b, l, row0=0, m=m_s, col0=da, n=2 * da, emit_bf16=True, emit_f32=True, tm=TM)
        u = _glu(h, w_in_b, l, col_a=3 * da, col_b=3 * da + dc, n=dc, tm=TM)
        attn = _attn_sample(q_s, kv_s, cache_k2, cache_v2, cache_bias, new_bias, l, n_heads, hd)
        halo = jnp.pad(state_conv[l], ((0, 0), (HALO - conv_state, 0), (0, 0)))
        cat = _conv_cat(u, halo, attn, w_dw, b_dw, g_conv_ln, b_conv_ln, g_attn_out, g_conv_out, l, CHUNK)
        x1, h2 = _outproj(cat, w_out_b, xs, mod, l, True, TM_OUT)
        xs = _ffn(h2, wg_b, wu_b, wd_b, x1, mod, l, True, TM)
        outs["conv_s"].append(u.reshape(dec_batch, dec_seq, dc)[:, dec_seq - conv_state:, :])
        outs["k_s"].append(kv_s32[:, :da].reshape(dec_batch, dec_seq, n_heads, hd))
        outs["v_s"].append(kv_s32[:, da:].reshape(dec_batch, dec_seq, n_heads, hd))

    stack = lambda name: jnp.stack(outs[name])
    return (xp.reshape(batch, seq, d), xs.reshape(dec_batch, dec_seq, d),
            stack("conv_p"), stack("k_p"), stack("v_p"), stack("conv_s"), stack("k_s"), stack("v_s"))
```

```python
import functools

import numpy as np
import jax
import jax.numpy as jnp
from jax import lax
from jax.experimental import pallas as pl
from jax.experimental.pallas import tpu as pltpu

F32 = jnp.float32
BF16 = jnp.bfloat16

EPS = 1e-6
NEG_INF = -1e30
CHUNK = 64
N_PAST_CHUNKS = 8
Q_CHUNKS = 4
BAND_CHUNKS = Q_CHUNKS + N_PAST_CHUNKS
N_MOD = 6
MOD_SH_M, MOD_SC_M, MOD_GT_M, MOD_SH_F, MOD_SC_F, MOD_GT_F = range(N_MOD)

V7X_VMEM_BYTES = 64 * 1024 * 1024
V7X_VMEM_REQUEST_CAP = 60 * 1024 * 1024
V7X_SPILL_ALLOWANCE = 4 * 1024 * 1024
V7X_LANES = 128
V7X_SUBLANES = 8
V7X_BF16_SUBLANES = 16

STRIP = 4 * V7X_BF16_SUBLANES
CONV_STRIP = 2 * V7X_BF16_SUBLANES
HALO = 32
NORM_UNROLL = 1

TM = 1024
TN_PROJ = 512
TN_GLU = 256
TM_OUT = 512
TK_OUT = 512
TF_FFN = 256
TM_EW = 512
TN_ADA = 512
TT_CONV_PROMPT = 128
ATTN_HEADS_PER_STEP = 4
MOD_ROWS = 48
PROMPT_MOD_ROW = 32


def _sigmoid(x):
    return 1.0 / (1.0 + jnp.exp(-x))


def _nbytes(shape, dtype):
    return int(np.prod(shape)) * jnp.dtype(dtype).itemsize


class _Specs:
    def __init__(self):
        self.bytes = 0

    def block(self, shape, index_map, dtype, buffers=2):
        real = tuple(s for s in shape if s is not None)
        self.bytes += buffers * _nbytes(real, dtype)
        if buffers == 1:
            return pl.BlockSpec(shape, index_map, pipeline_mode=pl.Buffered(1))
        return pl.BlockSpec(shape, index_map)

    def scratch(self, shape, dtype):
        self.bytes += _nbytes(shape, dtype)
        return pltpu.VMEM(shape, dtype)

    def params(self, semantics, temp_bytes=0):
        limit = min(self.bytes + temp_bytes + V7X_SPILL_ALLOWANCE, V7X_VMEM_REQUEST_CAP)
        return pltpu.CompilerParams(dimension_semantics=semantics, vmem_limit_bytes=limit)


def _ada_kernel(c_ref, w_ref, b_ref, gmpre_ref, gmpost_ref, gfpre_ref, gfpost_ref, o_ref, *, blocks_per_mod):
    which = pl.program_id(1) // blocks_per_mod
    c = c_ref[...]
    a = (c * _sigmoid(c)).astype(BF16)
    w = w_ref[...].astype(BF16)
    m = jnp.dot(a, w, preferred_element_type=F32) + b_ref[...]
    g_scale = jnp.where(which == MOD_SC_M, gmpre_ref[...], gfpre_ref[...])
    g_gate = jnp.where(which == MOD_GT_M, gmpost_ref[...], gfpost_ref[...])
    is_scale = jnp.logical_or(which == MOD_SC_M, which == MOD_SC_F)
    is_gate = jnp.logical_or(which == MOD_GT_M, which == MOD_GT_F)
    o_ref[...] = jnp.where(is_scale, g_scale * (1.0 + m), jnp.where(is_gate, g_gate * m, m))


def _ada(c_all, w_ada, b_ada, g_mix_pre, g_mix_post, g_ffn_pre, g_ffn_post):
    depth, d, n = w_ada.shape
    rows = c_all.shape[0]
    blocks_per_mod = d // TN_ADA
    sp = _Specs()
    gain = lambda: sp.block((None, 1, TN_ADA), lambda l, j: (l, 0, j % blocks_per_mod), F32)
    in_specs = [
        sp.block((rows, d), lambda l, j: (0, 0), F32),
        sp.block((None, d, TN_ADA), lambda l, j: (l, 0, j), F32),
        sp.block((None, 1, TN_ADA), lambda l, j: (l, 0, j), F32),
        gain(), gain(), gain(), gain(),
    ]
    out_spec = sp.block((None, rows, TN_ADA), lambda l, j: (l, 0, j), F32)
    return pl.pallas_call(
        functools.partial(_ada_kernel, blocks_per_mod=blocks_per_mod),
        grid=(depth, n // TN_ADA),
        in_specs=in_specs,
        out_specs=out_spec,
        out_shape=jax.ShapeDtypeStruct((depth, rows, n), F32),
        compiler_params=sp.params(("parallel", "parallel"), temp_bytes=_nbytes((d, TN_ADA), BF16)),
        name="ada_modulation",
    )(c_all, w_ada, b_ada.reshape(depth, 1, n), g_mix_pre, g_mix_post, g_ffn_pre, g_ffn_post)


def _mod_block(sp, layer, which, tm, d, per_stream, grid_rank):
    if per_stream:
        rows = tm // CHUNK
        assert rows % 8 == 0
        if grid_rank == 1:
            imap = lambda i: (layer, i, which)
        else:
            imap = lambda i, j: (layer, i, which)
    else:
        rows = 8
        blk = PROMPT_MOD_ROW // 8
        if grid_rank == 1:
            imap = lambda i: (layer, blk, which)
        else:
            imap = lambda i, j: (layer, blk, which)
    return sp.block((None, rows, d), imap, F32)


def _strip_loops(tm, per_stream, vec_refs, bc_ref, strip_fn):
    rows_per_vec = CHUNK if per_stream else tm
    n_inner = rows_per_vec // STRIP

    def outer(s, carry):
        for k, ref in enumerate(vec_refs):
            bc_ref[k] = jnp.broadcast_to(ref[pl.ds(s, 1), :], bc_ref.shape[1:])

        def inner(j, c2):
            strip_fn(pl.ds(pl.multiple_of(s * rows_per_vec + j * STRIP, STRIP), STRIP))
            return c2
        lax.fori_loop(0, n_inner, inner, 0, unroll=min(n_inner, NORM_UNROLL))
        return carry
    lax.fori_loop(0, tm // rows_per_vec, outer, 0)


def _split(x):
    return x.reshape(x.shape[0] // V7X_SUBLANES, V7X_SUBLANES, x.shape[-1])


def _merge(x3):
    return x3.reshape(x3.shape[0] * V7X_SUBLANES, x3.shape[-1])


def _inv_rms(x):
    return lax.rsqrt(jnp.mean(x * x, axis=-1, keepdims=True) + EPS)


def _prenorm_kernel(x_ref, sc_ref, sh_ref, h_ref, bc_ref, *, tm, per_stream):
    def strip(r):
        x = _split(x_ref[r, :])
        h_ref[r, :] = _merge(x * _inv_rms(x) * bc_ref[0] + bc_ref[1]).astype(BF16)
    _strip_loops(tm, per_stream, [sc_ref, sh_ref], bc_ref, strip)


def _prenorm(x, mod, layer, per_stream):
    m, d = x.shape
    tm = TM_EW
    sp = _Specs()
    in_specs = [
        sp.block((tm, d), lambda i: (i, 0), F32),
        _mod_block(sp, layer, MOD_SC_M, tm, d, per_stream, 1),
        _mod_block(sp, layer, MOD_SH_M, tm, d, per_stream, 1),
    ]
    out_spec = sp.block((tm, d), lambda i: (i, 0), BF16)
    return pl.pallas_call(
        functools.partial(_prenorm_kernel, tm=tm, per_stream=per_stream),
        grid=(m // tm,),
        in_specs=in_specs,
        out_specs=out_spec,
        out_shape=jax.ShapeDtypeStruct((m, d), BF16),
        scratch_shapes=[sp.scratch((2, V7X_SUBLANES, d), F32)],
        compiler_params=sp.params(("parallel",)),
        name="prenorm",
    )(x, mod, mod)


def _proj_kernel(h_ref, w_ref, *out_refs, emit_bf16, emit_f32):
    acc = jnp.dot(h_ref[...], w_ref[...].astype(BF16), preferred_element_type=F32)
    outs = list(out_refs)
    if emit_bf16:
        outs.pop(0)[...] = acc.astype(BF16)
    if emit_f32:
        outs.pop(0)[...] = acc


def _proj(h, w, layer, *, row0, m, col0, n, emit_bf16, emit_f32, tm):
    d = h.shape[1]
    tn = TN_PROJ
    assert row0 % tm == 0 and m % tm == 0 and col0 % tn == 0 and n % tn == 0
    rb, cb = row0 // tm, col0 // tn
    sp = _Specs()
    in_specs = [
        sp.block((tm, d), lambda i, j: (i + rb, 0), BF16),
        sp.block((None, d, tn), lambda i, j: (layer, 0, j + cb), w.dtype),
    ]
    out_specs, out_shape = [], []
    if emit_bf16:
        out_specs.append(sp.block((tm, tn), lambda i, j: (i, j), BF16))
        out_shape.append(jax.ShapeDtypeStruct((m, n), BF16))
    if emit_f32:
        out_specs.append(sp.block((tm, tn), lambda i, j: (i, j), F32))
        out_shape.append(jax.ShapeDtypeStruct((m, n), F32))
    return pl.pallas_call(
        functools.partial(_proj_kernel, emit_bf16=emit_bf16, emit_f32=emit_f32),
        grid=(m // tm, n // tn),
        in_specs=in_specs,
        out_specs=out_specs,
        out_shape=out_shape,
        compiler_params=sp.params(("parallel", "parallel"),
                                  temp_bytes=_nbytes((tm, tn), F32) + _nbytes((d, tn), BF16)),
        name="in_proj",
    )(h, w)


def _glu_kernel(h_ref, wa_ref, wb_ref, u_ref):
    h = h_ref[...]
    a = jnp.dot(h, wa_ref[...].astype(BF16), preferred_element_type=F32)
    b = jnp.dot(h, wb_ref[...].astype(BF16), preferred_element_type=F32)
    u_ref[...] = a * _sigmoid(b)


def _glu(h, w, layer, *, col_a, col_b, n, tm):
    m, d = h.shape
    tn = TN_GLU
    ca, cb = col_a // tn, col_b // tn
    sp = _Specs()
    in_specs = [
        sp.block((tm, d), lambda i, j: (i, 0), BF16),
        sp.block((None, d, tn), lambda i, j: (layer, 0, j + ca), w.dtype),
        sp.block((None, d, tn), lambda i, j: (layer, 0, j + cb), w.dtype),
    ]
    out_spec = sp.block((tm, tn), lambda i, j: (i, j), F32)
    return pl.pallas_call(
        _glu_kernel,
        grid=(m // tm, n // tn),
        in_specs=in_specs,
        out_specs=out_spec,
        out_shape=jax.ShapeDtypeStruct((m, n), F32),
        compiler_params=sp.params(("parallel", "parallel"),
                                  temp_bytes=3 * _nbytes((tm, tn), F32) + 2 * _nbytes((d, tn), BF16)),
        name="in_proj_glu",
    )(h, w, w)


def _softmax_pv(parts):
    m = None
    for s, _ in parts:
        mi = jnp.max(s, axis=-1, keepdims=True)
        m = mi if m is None else jnp.maximum(m, mi)
    l, o = None, None
    for s, v in parts:
        p = jnp.exp(s - m)
        li = jnp.sum(p, axis=-1, keepdims=True)
        oi = jnp.dot(p.astype(BF16), v, preferred_element_type=F32)
        l = li if l is None else l + li
        o = oi if o is None else o + oi
    return o / l


def _qk(q, k):
    return lax.dot_general(q, k, (((1,), (1,)), ((), ())), preferred_element_type=F32)


def _attn_prompt_kernel(q_ref, k_ref, v_ref, b_ref, o_ref, *, scale, hd, heads):
    g = pl.program_id(1)
    rows = Q_CHUNKS * CHUNK

    def run(k0, n_keys, b_off):
        for h in range(heads):
            c = slice(h * hd, (h + 1) * hd)
            k = k_ref[pl.ds(k0, n_keys), c]
            v = v_ref[pl.ds(k0, n_keys), c]
            s = _qk(q_ref[:, c], k) * scale + b_ref[h, :, b_off:b_off + n_keys]
            o_ref[:, c] = _softmax_pv([(s, v)])

    n_lead = N_PAST_CHUNKS // Q_CHUNKS
    for lead in range(n_lead):
        keys = (lead + 1) * rows
        pl.when(g == lead)(functools.partial(run, 0, keys, BAND_CHUNKS * CHUNK - keys))

    @pl.when(g >= n_lead)
    def _():
        run(pl.multiple_of((g - n_lead) * rows, rows), BAND_CHUNKS * CHUNK, 0)


def _attn_prompt(qkv, bias, layer, n_heads, hd):
    t = qkv.shape[0]
    rows = Q_CHUNKS * CHUNK
    keys = BAND_CHUNKS * CHUNK
    heads = ATTN_HEADS_PER_STEP
    assert n_heads % heads == 0
    n_groups = n_heads // heads
    width = heads * hd
    sp = _Specs()
    in_specs = [
        sp.block((rows, width), lambda h, g: (g, h), BF16),
        sp.block((t, width), lambda h, g: (0, n_groups + h), BF16),
        sp.block((t, width), lambda h, g: (0, 2 * n_groups + h), BF16),
        sp.block((None, heads, rows, keys), lambda h, g: (layer, h, 0, 0), F32),
    ]
    out_spec = sp.block((rows, width), lambda h, g: (g, h), F32)
    return pl.pallas_call(
        functools.partial(_attn_prompt_kernel, scale=hd ** -0.5, hd=hd, heads=heads),
        grid=(n_groups, t // rows),
        in_specs=in_specs,
        out_specs=out_spec,
        out_shape=jax.ShapeDtypeStruct((t, n_heads * hd), F32),
        compiler_params=sp.params(("parallel", "parallel"), temp_bytes=6 * _nbytes((rows, keys), F32)),
        name="attn_prompt",
    )(qkv, qkv, qkv, bias)


def _attn_sample_kernel(q_ref, kn_ref, vn_ref, kc_ref, vc_ref, bc_ref, bn_ref, o_ref, *, scale, n_heads, hd):
    keep = kc_ref.shape[0] // n_heads
    for h in range(n_heads):
        c = slice(h * hd, (h + 1) * hd)
        frames = pl.ds(h, keep, stride=n_heads)
        q = q_ref[:, c]
        s_cache = _qk(q, kc_ref[frames, :].astype(BF16)) * scale + bc_ref[h]
        s_new = _qk(q, kn_ref[:, c]) * scale + bn_ref[h]
        o_ref[:, c] = _softmax_pv([(s_cache, vc_ref[frames, :].astype(BF16)), (s_new, vn_ref[:, c])])


def _attn_sample(q, kv, cache_k, cache_v, bias_cache, bias_new, layer, n_heads, hd):
    m, da = q.shape
    n_streams = m // CHUNK
    keep = cache_k.shape[2] // n_heads
    sp = _Specs()
    in_specs = [
        sp.block((CHUNK, da), lambda b: (b, 0), BF16),
        sp.block((CHUNK, da), lambda b: (b, 0), BF16),
        sp.block((CHUNK, da), lambda b: (b, 1), BF16),
        sp.block((None, None, keep * n_heads, hd), lambda b: (layer, b, 0, 0), F32),
        sp.block((None, None, keep * n_heads, hd), lambda b: (layer, b, 0, 0), F32),
        sp.block((None, n_heads, CHUNK, keep), lambda b: (layer, 0, 0, 0), F32),
        sp.block((None, n_heads, CHUNK, CHUNK), lambda b: (layer, 0, 0, 0), F32),
    ]
    out_spec = sp.block((CHUNK, da), lambda b: (b, 0), F32)
    return pl.pallas_call(
        functools.partial(_attn_sample_kernel, scale=hd ** -0.5, n_heads=n_heads, hd=hd),
        grid=(n_streams,),
        in_specs=in_specs,
        out_specs=out_spec,
        out_shape=jax.ShapeDtypeStruct((m, da), F32),
        compiler_params=sp.params(("parallel",), temp_bytes=4 * 1024 * 1024),
        name="attn_sample",
    )(q, kv, kv, cache_k, cache_v, bias_cache, bias_new)


def _bias_tables(rel_bias):
    depth, n_heads, n_rel = rel_bias.shape
    max_rel = (n_rel - 1) // 2
    rows, cols = Q_CHUNKS * CHUNK, BAND_CHUNKS * CHUNK
    p = rows + cols - 1
    x = np.arange(p)
    rel = np.where(x < cols, -x, p - x) + N_PAST_CHUNKS * CHUNK
    diag = rel_bias[:, :, np.clip(rel, -max_rel, max_rel) + max_rel]
    a = jnp.tile(diag, (1, 1, rows))[:, :, :rows * (p - 1)].reshape(depth, n_heads, rows, p - 1)
    full = a[..., :cols]
    qi = np.arange(rows)[:, None] // CHUNK
    kj = np.arange(cols)[None, :] // CHUNK
    valid = (kj >= qi) & (kj <= qi + N_PAST_CHUNKS)
    band = jnp.where(valid, full, NEG_INF)
    keep = N_PAST_CHUNKS * CHUNK
    return band, full[:, :, :CHUNK, :keep], full[:, :, :CHUNK, keep:keep + CHUNK]


def _conv_cat_kernel(u_ref, hist_ref, attn_ref, wdw_ref, bdw_ref, gln_ref, bln_ref, gao_ref, gco_ref,
                     cat_ref, xs_ref, y_ref, bc_ref, *, tt, conv_width, hist_is_state):
    da = attn_ref.shape[1]
    dc = u_ref.shape[1]
    if hist_is_state:
        n_hist = hist_ref.shape[0]
        xs_ref[0, 0:V7X_SUBLANES, :] = jnp.zeros((V7X_SUBLANES, dc), F32)
        xs_ref[0, HALO - n_hist:HALO, :] = hist_ref[...]
    else:
        xs_ref[0, 0:HALO, :] = jnp.where(pl.program_id(0) > 0, hist_ref[...], 0.0)
    xs_ref[0, HALO:HALO + tt, :] = u_ref[...]
    first = HALO - (conv_width - 1)
    n_shift = HALO + tt - V7X_SUBLANES
    for r in range(1, V7X_SUBLANES):
        xs_ref[r, 0:n_shift, :] = xs_ref[0, r:r + n_shift, :]
    rc_rows = 64
    for rc in range(tt // rc_rows):
        for cb in range(dc // V7X_LANES):
            c = slice(cb * V7X_LANES, (cb + 1) * V7X_LANES)
            acc = None
            for w in range(conv_width):
                r0 = rc * rc_rows + (first + w) // V7X_SUBLANES * V7X_SUBLANES
                term = xs_ref[(first + w) % V7X_SUBLANES, r0:r0 + rc_rows, c] * wdw_ref[w:w + 1, c]
                acc = term if acc is None else acc + term
            y_ref[rc * rc_rows:(rc + 1) * rc_rows, c] = acc + bdw_ref[:, c]

    for k, ref in enumerate((gln_ref, bln_ref, gco_ref, gao_ref)):
        bc_ref[k] = jnp.broadcast_to(ref[...], bc_ref.shape[1:])

    def strip(j, carry):
        r = pl.ds(pl.multiple_of(j * CONV_STRIP, CONV_STRIP), CONV_STRIP)
        y = _split(y_ref[r, :])
        yc = y - jnp.mean(y, axis=-1, keepdims=True)
        yn = yc * _inv_rms(yc) * bc_ref[0] + bc_ref[1]
        z = yn * _sigmoid(yn)
        cat_ref[r, da:da + dc] = _merge(z * _inv_rms(z) * bc_ref[2]).astype(BF16)
        a = _split(attn_ref[r, :])
        cat_ref[r, 0:da] = _merge(a * _inv_rms(a) * bc_ref[3]).astype(BF16)
        return carry
    n_strips = tt // CONV_STRIP
    lax.fori_loop(0, n_strips, strip, 0, unroll=2 if n_strips > 2 else 1)


def _conv_cat(u, state, attn, w_dw, b_dw, g_ln, b_ln, g_ao, g_co, layer, tt):
    m, dc = u.shape
    da = attn.shape[1]
    conv_width = w_dw.shape[1]
    assert conv_width - 1 <= HALO and tt % HALO == 0
    assert da == dc
    sp = _Specs()
    vec = lambda width: sp.block((None, 1, width), lambda i: (layer, 0, 0), F32)
    if state is None:
        hist, hist_spec = u, sp.block((HALO, dc), lambda i: (jnp.maximum(i * (tt // HALO) - 1, 0), 0), F32)
    else:
        assert state.shape[1:] == (m // tt, conv_width - 1, dc)
        hist, hist_spec = state, sp.block((None, None, conv_width - 1, dc), lambda i: (layer, i, 0, 0), F32)
    in_specs = [
        sp.block((tt, dc), lambda i: (i, 0), F32),
        hist_spec,
        sp.block((tt, da), lambda i: (i, 0), F32),
        sp.block((None, conv_width, dc), lambda i: (layer, 0, 0), F32),
        vec(dc), vec(dc), vec(dc), vec(da), vec(dc),
    ]
    out_spec = sp.block((tt, da + dc), lambda i: (i, 0), BF16)
    scratch = [sp.scratch((V7X_SUBLANES, HALO + tt, dc), F32), sp.scratch((tt, dc), F32),
               sp.scratch((4, V7X_SUBLANES, dc), F32)]
    return pl.pallas_call(
        functools.partial(_conv_cat_kernel, tt=tt, conv_width=conv_width, hist_is_state=state is not None),
        grid=(m // tt,),
        in_specs=in_specs,
        out_specs=out_spec,
        out_shape=jax.ShapeDtypeStruct((m, da + dc), BF16),
        scratch_shapes=scratch,
        compiler_params=sp.params(("parallel",), temp_bytes=4 * _nbytes((tt, dc), F32)),
        name="conv_cat",
    )(u, hist, attn, w_dw, b_dw, g_ln, b_ln, g_ao, g_co)


def _outproj_kernel(cat_ref, w_ref, x_ref, gt_ref, sc_ref, sh_ref, x1_ref, h2_ref, bc_ref, *, tm, nk, per_stream):
    k = pl.program_id(1)

    @pl.when(k == 0)
    def _():
        x1_ref[...] = jnp.dot(cat_ref[...], w_ref[...], preferred_element_type=F32)

    @pl.when(k > 0)
    def _():
        x1_ref[...] += jnp.dot(cat_ref[...], w_ref[...], preferred_element_type=F32)

    @pl.when(k == nk - 1)
    def _():
        def strip(r):
            mixed = _split(x1_ref[r, :])
            x1 = _split(x_ref[r, :]) + mixed * _inv_rms(mixed) * bc_ref[0]
            x1_ref[r, :] = _merge(x1)
            h2_ref[r, :] = _merge(x1 * _inv_rms(x1) * bc_ref[1] + bc_ref[2]).astype(BF16)
        _strip_loops(tm, per_stream, [gt_ref, sc_ref, sh_ref], bc_ref, strip)


def _outproj(cat, w_out, x, mod, layer, per_stream, tm):
    m, dmix = cat.shape
    d = w_out.shape[2]
    tk = TK_OUT
    nk = dmix // tk
    sp = _Specs()
    in_specs = [
        sp.block((tm, tk), lambda i, k: (i, k), BF16),
        sp.block((None, tk, d), lambda i, k: (layer, k, 0), BF16),
        sp.block((tm, d), lambda i, k: (i, 0), F32),
        _mod_block(sp, layer, MOD_GT_M, tm, d, per_stream, 2),
        _mod_block(sp, layer, MOD_SC_F, tm, d, per_stream, 2),
        _mod_block(sp, layer, MOD_SH_F, tm, d, per_stream, 2),
    ]
    out_specs = [
        sp.block((tm, d), lambda i, k: (i, 0), F32),
        sp.block((tm, d), lambda i, k: (i, 0), BF16),
    ]
    return pl.pallas_call(
        functools.partial(_outproj_kernel, tm=tm, nk=nk, per_stream=per_stream),
        grid=(m // tm, nk),
        in_specs=in_specs,
        out_specs=out_specs,
        out_shape=[jax.ShapeDtypeStruct((m, d), F32), jax.ShapeDtypeStruct((m, d), BF16)],
        scratch_shapes=[sp.scratch((3, V7X_SUBLANES, d), F32)],
        compiler_params=sp.params(("parallel", "arbitrary")),
        name="out_proj",
    )(cat, w_out, x, mod, mod, mod)


def _ffn_kernel(h_ref, wg_ref, wu_ref, wd_ref, x1_ref, gt_ref, x2_ref, bc_ref, *, tm, nf, per_stream):
    f = pl.program_id(1)

    @pl.when(f == 0)
    def _():
        x2_ref[...] = jnp.zeros_like(x2_ref)

    h = h_ref[...]
    g = jnp.dot(h, wg_ref[...], preferred_element_type=F32)
    u = jnp.dot(h, wu_ref[...], preferred_element_type=F32)
    a = (g * _sigmoid(g) * u).astype(BF16)
    x2_ref[...] += jnp.dot(a, wd_ref[...], preferred_element_type=F32)

    @pl.when(f == nf - 1)
    def _():
        def strip(r):
            y = _split(x2_ref[r, :])
            x2_ref[r, :] = _merge(_split(x1_ref[r, :]) + y * _inv_rms(y) * bc_ref[0])
        _strip_loops(tm, per_stream, [gt_ref], bc_ref, strip)


def _ffn(h2, wg, wu, wd, x1, mod, layer, per_stream, tm):
    m, d = h2.shape
    dff = wg.shape[2]
    tf = TF_FFN
    nf = dff // tf
    assert dff % tf == 0
    sp = _Specs()
    in_specs = [
        sp.block((tm, d), lambda i, f: (i, 0), BF16, buffers=1),
        sp.block((None, d, tf), lambda i, f: (layer, 0, f), BF16),
        sp.block((None, d, tf), lambda i, f: (layer, 0, f), BF16),
        sp.block((None, tf, d), lambda i, f: (layer, f, 0), BF16),
        sp.block((tm, d), lambda i, f: (i, 0), F32, buffers=1),
        _mod_block(sp, layer, MOD_GT_F, tm, d, per_stream, 2),
    ]
    out_spec = sp.block((tm, d), lambda i, f: (i, 0), F32, buffers=1)
    return pl.pallas_call(
        functools.partial(_ffn_kernel, tm=tm, nf=nf, per_stream=per_stream),
        grid=(m // tm, nf),
        in_specs=in_specs,
        out_specs=out_spec,
        out_shape=jax.ShapeDtypeStruct((m, d), F32),
        scratch_shapes=[sp.scratch((1, V7X_SUBLANES, d), F32)],
        compiler_params=sp.params(("parallel", "arbitrary"), temp_bytes=4 * _nbytes((tm, tf), F32)),
        name="ffn",
    )(h2, wg, wu, wd, x1, mod)


def kernel(x_prompt, x_sample, cache_k, cache_v, state_conv, c_prompt, c_sample, w_ada, b_ada, g_mix_pre,
           g_mix_post, w_in, rel_bias, w_dw, b_dw, g_conv_ln, b_conv_ln, g_attn_out, g_conv_out, w_out,
           g_ffn_pre, g_ffn_post, w_ffn_gate, w_ffn_up, w_ffn_down):
    batch, seq, d = x_prompt.shape
    dec_batch, dec_seq, _ = x_sample.shape
    depth, _, keep, n_heads, hd = cache_k.shape
    da = n_heads * hd
    dc = state_conv.shape[3]
    conv_state = state_conv.shape[2]
    assert batch == 1 and dec_seq == CHUNK and keep == N_PAST_CHUNKS * CHUNK and dec_batch == PROMPT_MOD_ROW
    assert seq % (Q_CHUNKS * CHUNK) == 0 and seq >= keep
    m_s = dec_batch * dec_seq

    vec = lambda a: a.reshape(depth, 1, a.shape[-1])
    g_mix_pre, g_mix_post, g_ffn_pre, g_ffn_post = map(vec, (g_mix_pre, g_mix_post, g_ffn_pre, g_ffn_post))
    b_dw, g_conv_ln, b_conv_ln, g_attn_out, g_conv_out = map(vec, (b_dw, g_conv_ln, b_conv_ln, g_attn_out, g_conv_out))

    c_all = jnp.concatenate(
        [c_sample, c_prompt, jnp.zeros((MOD_ROWS - dec_batch - batch, d), F32)], axis=0)
    mod = _ada(c_all, w_ada, b_ada, g_mix_pre, g_mix_post, g_ffn_pre, g_ffn_post)

    w_out_b = w_out.astype(BF16)
    wg_b, wu_b, wd_b = w_ffn_gate.astype(BF16), w_ffn_up.astype(BF16), w_ffn_down.astype(BF16)
    band_bias, cache_bias, new_bias = _bias_tables(rel_bias)
    cache_k2 = cache_k.reshape(depth, dec_batch, keep * n_heads, hd)
    cache_v2 = cache_v.reshape(depth, dec_batch, keep * n_heads, hd)

    xp = x_prompt.reshape(seq, d)
    xs = x_sample.reshape(m_s, d)
    outs = {k: [] for k in ("conv_p", "k_p", "v_p", "conv_s", "k_s", "v_s")}
    tt_p = TT_CONV_PROMPT
    for l in range(depth):
        h = _prenorm(xp, mod, l, per_stream=False)
        (qkv,) = _proj(h, w_in, l, row0=0, m=seq, col0=0, n=3 * da, emit_bf16=True, emit_f32=False, tm=TM)
        (kv_tail,) = _proj(h, w_in, l, row0=seq - keep, m=keep, col0=da, n=2 * da,
                           emit_bf16=False, emit_f32=True, tm=keep)
        u = _glu(h, w_in, l, col_a=3 * da, col_b=3 * da + dc, n=dc, tm=TM)
        attn = _attn_prompt(qkv, band_bias, l, n_heads, hd)
        cat = _conv_cat(u, None, attn, w_dw, b_dw, g_conv_ln, b_conv_ln, g_attn_out, g_conv_out, l, tt_p)
        x1, h2 = _outproj(cat, w_out_b, xp, mod, l, False, TM_OUT)
        xp = _ffn(h2, wg_b, wu_b, wd_b, x1, mod, l, False, TM)
        outs["conv_p"].append(u[seq - conv_state:].reshape(batch, conv_state, dc))
        outs["k_p"].append(kv_tail[:, :da].reshape(batch, keep, n_heads, hd))
        outs["v_p"].append(kv_tail[:, da:].reshape(batch, keep, n_heads, hd))

        h = _prenorm(xs, mod, l, per_stream=True)
        (q_s,) = _proj(h, w_in, l, row0=0, m=m_s, col0=0, n=da, emit_bf16=True, emit_f32=False, tm=TM)
        kv_s, kv_s32 = _proj(h, w_in, l, row0=0, m=m_s, col0=da, n=2 * da, emit_bf16=True, emit_f32=True, tm=TM)
        u = _glu(h, w_in, l, col_a=3 * da, col_b=3 * da + dc, n=dc, tm=TM)
        attn = _attn_sample(q_s, kv_s, cache_k2, cache_v2, cache_bias, new_bias, l, n_heads, hd)
        cat = _conv_cat(u, state_conv, attn, w_dw, b_dw, g_conv_ln, b_conv_ln, g_attn_out, g_conv_out, l, CHUNK)
        x1, h2 = _outproj(cat, w_out_b, xs, mod, l, True, TM_OUT)
        xs = _ffn(h2, wg_b, wu_b, wd_b, x1, mod, l, True, TM)
        outs["conv_s"].append(u.reshape(dec_batch, dec_seq, dc)[:, dec_seq - conv_state:, :])
        outs["k_s"].append(kv_s32[:, :da].reshape(dec_batch, dec_seq, n_heads, hd))
        outs["v_s"].append(kv_s32[:, da:].reshape(dec_batch, dec_seq, n_heads, hd))

    stack = lambda name: jnp.stack(outs[name])
    return (xp.reshape(batch, seq, d), xs.reshape(dec_batch, dec_seq, d),
            stack("conv_p"), stack("k_p"), stack("v_p"), stack("conv_s"), stack("k_s"), stack("v_s"))
```

```python
import functools

import numpy as np
import jax
import jax.numpy as jnp
from jax import lax
from jax.experimental import pallas as pl
from jax.experimental.pallas import tpu as pltpu

F32 = jnp.float32
BF16 = jnp.bfloat16

EPS = 1e-6
NEG_INF = -1e30
CHUNK = 64
N_PAST_CHUNKS = 8
Q_CHUNKS = 4
BAND_CHUNKS = Q_CHUNKS + N_PAST_CHUNKS
N_MOD = 6
MOD_SH_M, MOD_SC_M, MOD_GT_M, MOD_SH_F, MOD_SC_F, MOD_GT_F = range(N_MOD)

V7X_VMEM_BYTES = 64 * 1024 * 1024
V7X_VMEM_REQUEST_CAP = 60 * 1024 * 1024
V7X_SPILL_ALLOWANCE = 4 * 1024 * 1024
V7X_LANES = 128
V7X_SUBLANES = 8
V7X_BF16_SUBLANES = 16

STRIP = 4 * V7X_BF16_SUBLANES
CONV_STRIP = 2 * V7X_BF16_SUBLANES
HALO = 32
NORM_UNROLL = 1

TM = 1024
TN_PROJ = 512
TN_GLU = 256
TM_OUT = 512
TK_OUT = 512
TF_FFN = 256
FFN_CHUNK = 128
TM_EW = 512
TN_ADA = 512
TT_CONV_PROMPT = 128
SIDE_BLOCKS = 64
SIDE_ROW_BLOCKS, SIDE_COL_BLOCKS = 16, 4
ATTN_HEADS_PER_STEP = 4
MOD_ROWS = 48
PROMPT_MOD_ROW = 32


def _sigmoid(x):
    return 1.0 / (1.0 + jnp.exp(-x))


def _nbytes(shape, dtype):
    return int(np.prod(shape)) * jnp.dtype(dtype).itemsize


class _Specs:
    def __init__(self):
        self.bytes = 0

    def block(self, shape, index_map, dtype, buffers=2):
        real = tuple(s for s in shape if s is not None)
        self.bytes += buffers * _nbytes(real, dtype)
        if buffers == 1:
            return pl.BlockSpec(shape, index_map, pipeline_mode=pl.Buffered(1))
        return pl.BlockSpec(shape, index_map)

    def scratch(self, shape, dtype):
        self.bytes += _nbytes(shape, dtype)
        return pltpu.VMEM(shape, dtype)

    def params(self, semantics, temp_bytes=0):
        limit = min(self.bytes + temp_bytes + V7X_SPILL_ALLOWANCE, V7X_VMEM_REQUEST_CAP)
        return pltpu.CompilerParams(dimension_semantics=semantics, vmem_limit_bytes=limit)


def _ada_kernel(c_ref, w_ref, b_ref, gmpre_ref, gmpost_ref, gfpre_ref, gfpost_ref, o_ref, *, blocks_per_mod):
    which = pl.program_id(1) // blocks_per_mod
    c = c_ref[...]
    a = (c * _sigmoid(c)).astype(BF16)
    w = w_ref[...].astype(BF16)
    m = jnp.dot(a, w, preferred_element_type=F32) + b_ref[...]
    g_scale = jnp.where(which == MOD_SC_M, gmpre_ref[...], gfpre_ref[...])
    g_gate = jnp.where(which == MOD_GT_M, gmpost_ref[...], gfpost_ref[...])
    is_scale = jnp.logical_or(which == MOD_SC_M, which == MOD_SC_F)
    is_gate = jnp.logical_or(which == MOD_GT_M, which == MOD_GT_F)
    o_ref[...] = jnp.where(is_scale, g_scale * (1.0 + m), jnp.where(is_gate, g_gate * m, m))


def _ada(c_all, w_ada, b_ada, g_mix_pre, g_mix_post, g_ffn_pre, g_ffn_post):
    depth, d, n = w_ada.shape
    rows = c_all.shape[0]
    blocks_per_mod = d // TN_ADA
    sp = _Specs()
    gain = lambda: sp.block((None, 1, TN_ADA), lambda l, j: (l, 0, j % blocks_per_mod), F32)
    in_specs = [
        sp.block((rows, d), lambda l, j: (0, 0), F32),
        sp.block((None, d, TN_ADA), lambda l, j: (l, 0, j), F32),
        sp.block((None, 1, TN_ADA), lambda l, j: (l, 0, j), F32),
        gain(), gain(), gain(), gain(),
    ]
    out_spec = sp.block((None, rows, TN_ADA), lambda l, j: (l, 0, j), F32)
    return pl.pallas_call(
        functools.partial(_ada_kernel, blocks_per_mod=blocks_per_mod),
        grid=(depth, n // TN_ADA),
        in_specs=in_specs,
        out_specs=out_spec,
        out_shape=jax.ShapeDtypeStruct((depth, rows, n), F32),
        compiler_params=sp.params(("parallel", "parallel"), temp_bytes=_nbytes((d, TN_ADA), BF16)),
        name="ada_modulation",
    )(c_all, w_ada, b_ada.reshape(depth, 1, n), g_mix_pre, g_mix_post, g_ffn_pre, g_ffn_post)


def _mod_block(sp, layer, which, tm, d, per_stream, grid_rank):
    if per_stream:
        rows = tm // CHUNK
        assert rows % 8 == 0
        if grid_rank == 1:
            imap = lambda i: (layer, i, which)
        else:
            imap = lambda i, j: (layer, i, which)
    else:
        rows = 8
        blk = PROMPT_MOD_ROW // 8
        if grid_rank == 1:
            imap = lambda i: (layer, blk, which)
        else:
            imap = lambda i, j: (layer, blk, which)
    return sp.block((None, rows, d), imap, F32)


def _strip_loops(tm, per_stream, vec_refs, bc_ref, strip_fn):
    rows_per_vec = CHUNK if per_stream else tm
    n_inner = rows_per_vec // STRIP

    def outer(s, carry):
        for k, ref in enumerate(vec_refs):
            bc_ref[k] = jnp.broadcast_to(ref[pl.ds(s, 1), :], bc_ref.shape[1:])

        def inner(j, c2):
            strip_fn(pl.ds(pl.multiple_of(s * rows_per_vec + j * STRIP, STRIP), STRIP))
            return c2
        lax.fori_loop(0, n_inner, inner, 0, unroll=min(n_inner, NORM_UNROLL))
        return carry
    lax.fori_loop(0, tm // rows_per_vec, outer, 0)


def _split(x):
    return x.reshape(x.shape[0] // V7X_SUBLANES, V7X_SUBLANES, x.shape[-1])


def _merge(x3):
    return x3.reshape(x3.shape[0] * V7X_SUBLANES, x3.shape[-1])


def _inv_rms(x):
    return lax.rsqrt(jnp.mean(x * x, axis=-1, keepdims=True) + EPS)


def _prenorm_kernel(x_ref, sc_ref, sh_ref, h_ref, bc_ref, *, tm, per_stream):
    def strip(r):
        x = _split(x_ref[r, :])
        h_ref[r, :] = _merge(x * _inv_rms(x) * bc_ref[0] + bc_ref[1]).astype(BF16)
    _strip_loops(tm, per_stream, [sc_ref, sh_ref], bc_ref, strip)


def _prenorm(x, mod, layer, per_stream):
    m, d = x.shape
    tm = TM_EW
    sp = _Specs()
    in_specs = [
        sp.block((tm, d), lambda i: (i, 0), F32),
        _mod_block(sp, layer, MOD_SC_M, tm, d, per_stream, 1),
        _mod_block(sp, layer, MOD_SH_M, tm, d, per_stream, 1),
    ]
    out_spec = sp.block((tm, d), lambda i: (i, 0), BF16)
    return pl.pallas_call(
        functools.partial(_prenorm_kernel, tm=tm, per_stream=per_stream),
        grid=(m // tm,),
        in_specs=in_specs,
        out_specs=out_spec,
        out_shape=jax.ShapeDtypeStruct((m, d), BF16),
        scratch_shapes=[sp.scratch((2, V7X_SUBLANES, d), F32)],
        compiler_params=sp.params(("parallel",)),
        name="prenorm",
    )(x, mod, mod)


class _SideConvert:
    def __init__(self, src, layer, block):
        self.src, self.layer, self.block = src, layer, block
        rows, cols = src.shape[1:]
        br, bc = block
        assert rows % br == 0 and cols % bc == 0 and br % V7X_BF16_SUBLANES == 0 and bc % V7X_LANES == 0
        self.per_row = cols // bc
        self.n_blocks = (rows // br) * self.per_row

    def specs(self, sp, steps_per_row, n_steps):
        assert n_steps >= self.n_blocks
        last, per_row, layer = self.n_blocks - 1, self.per_row, self.layer

        def block_of(i, j):
            b = jnp.minimum(i * steps_per_row + j, last)
            return b // per_row, b % per_row
        in_spec = sp.block((None,) + self.block, lambda i, j: (layer,) + block_of(i, j), self.src.dtype)
        out_spec = sp.block(self.block, block_of, BF16)
        return in_spec, out_spec, jax.ShapeDtypeStruct(self.src.shape[1:], BF16)


def _side_specs(sp, side, steps_per_row, n_steps):
    specs = [c.specs(sp, steps_per_row, n_steps) for c in side]
    return [s[0] for s in specs], [s[1] for s in specs], [s[2] for s in specs], [c.src for c in side]


def _run_side_converts(refs, n_side):
    for src_ref, dst_ref in zip(refs[:n_side], refs[len(refs) - n_side:]):
        dst_ref[...] = src_ref[...].astype(BF16)


def _proj_kernel(h_ref, w_ref, *refs, emit_bf16, emit_f32, n_side):
    acc = jnp.dot(h_ref[...], w_ref[...].astype(BF16), preferred_element_type=F32)
    outs = list(refs[n_side:])
    if emit_bf16:
        outs.pop(0)[...] = acc.astype(BF16)
    if emit_f32:
        outs.pop(0)[...] = acc
    _run_side_converts(refs, n_side)


def _proj(h, w, layer, *, row0, m, col0, n, emit_bf16, emit_f32, tm, side=()):
    d = h.shape[1]
    tn = TN_PROJ
    assert row0 % tm == 0 and m % tm == 0 and col0 % tn == 0 and n % tn == 0
    rb, cb = row0 // tm, col0 // tn
    grid = (m // tm, n // tn)
    sp = _Specs()
    side_in, side_out, side_shape, side_args = _side_specs(sp, side, grid[1], grid[0] * grid[1])
    in_specs = [
        sp.block((tm, d), lambda i, j: (i + rb, 0), BF16),
        sp.block((None, d, tn), lambda i, j: (layer, 0, j + cb), w.dtype),
    ] + side_in
    out_specs, out_shape = [], []
    if emit_bf16:
        out_specs.append(sp.block((tm, tn), lambda i, j: (i, j), BF16))
        out_shape.append(jax.ShapeDtypeStruct((m, n), BF16))
    if emit_f32:
        out_specs.append(sp.block((tm, tn), lambda i, j: (i, j), F32))
        out_shape.append(jax.ShapeDtypeStruct((m, n), F32))
    return pl.pallas_call(
        functools.partial(_proj_kernel, emit_bf16=emit_bf16, emit_f32=emit_f32, n_side=len(side)),
        grid=grid,
        in_specs=in_specs,
        out_specs=out_specs + side_out,
        out_shape=out_shape + side_shape,
        compiler_params=sp.params(("arbitrary", "arbitrary"),
                                  temp_bytes=_nbytes((tm, tn), F32) + _nbytes((d, tn), BF16)),
        name="in_proj",
    )(h, w, *side_args)


def _glu_kernel(h_ref, wa_ref, wb_ref, *refs, n_side):
    h = h_ref[...]
    a = jnp.dot(h, wa_ref[...].astype(BF16), preferred_element_type=F32)
    b = jnp.dot(h, wb_ref[...].astype(BF16), preferred_element_type=F32)
    refs[n_side][...] = a * _sigmoid(b)
    _run_side_converts(refs, n_side)


def _glu(h, w, layer, *, col_a, col_b, n, tm, side=()):
    m, d = h.shape
    tn = TN_GLU
    ca, cb = col_a // tn, col_b // tn
    grid = (m // tm, n // tn)
    sp = _Specs()
    side_in, side_out, side_shape, side_args = _side_specs(sp, side, grid[1], grid[0] * grid[1])
    in_specs = [
        sp.block((tm, d), lambda i, j: (i, 0), BF16),
        sp.block((None, d, tn), lambda i, j: (layer, 0, j + ca), w.dtype),
        sp.block((None, d, tn), lambda i, j: (layer, 0, j + cb), w.dtype),
    ] + side_in
    out_spec = sp.block((tm, tn), lambda i, j: (i, j), F32)
    return pl.pallas_call(
        functools.partial(_glu_kernel, n_side=len(side)),
        grid=grid,
        in_specs=in_specs,
        out_specs=[out_spec] + side_out,
        out_shape=[jax.ShapeDtypeStruct((m, n), F32)] + side_shape,
        compiler_params=sp.params(("arbitrary", "arbitrary"),
                                  temp_bytes=3 * _nbytes((tm, tn), F32) + 2 * _nbytes((d, tn), BF16)),
        name="in_proj_glu",
    )(h, w, w, *side_args)


def _softmax_pv(parts):
    m = None
    for s, _ in parts:
        mi = jnp.max(s, axis=-1, keepdims=True)
        m = mi if m is None else jnp.maximum(m, mi)
    l, o = None, None
    for s, v in parts:
        p = jnp.exp(s - m)
        li = jnp.sum(p, axis=-1, keepdims=True)
        oi = jnp.dot(p.astype(BF16), v, preferred_element_type=F32)
        l = li if l is None else l + li
        o = oi if o is None else o + oi
    return o / l


def _qk(q, k):
    return lax.dot_general(q, k, (((1,), (1,)), ((), ())), preferred_element_type=F32)


def _attn_prompt_kernel(q_ref, k_ref, v_ref, b_ref, o_ref, *, scale, hd, heads):
    g = pl.program_id(1)
    rows = Q_CHUNKS * CHUNK

    def run(k0, n_keys, b_off):
        for h in range(heads):
            c = slice(h * hd, (h + 1) * hd)
            k = k_ref[pl.ds(k0, n_keys), c]
            v = v_ref[pl.ds(k0, n_keys), c]
            s = _qk(q_ref[:, c], k) * scale + b_ref[h, :, b_off:b_off + n_keys]
            o_ref[:, c] = _softmax_pv([(s, v)])

    n_lead = N_PAST_CHUNKS // Q_CHUNKS
    for lead in range(n_lead):
        keys = (lead + 1) * rows
        pl.when(g == lead)(functools.partial(run, 0, keys, BAND_CHUNKS * CHUNK - keys))

    @pl.when(g >= n_lead)
    def _():
        run(pl.multiple_of((g - n_lead) * rows, rows), BAND_CHUNKS * CHUNK, 0)


def _attn_prompt(qkv, bias, layer, n_heads, hd):
    t = qkv.shape[0]
    rows = Q_CHUNKS * CHUNK
    keys = BAND_CHUNKS * CHUNK
    heads = ATTN_HEADS_PER_STEP
    assert n_heads % heads == 0
    n_groups = n_heads // heads
    width = heads * hd
    sp = _Specs()
    in_specs = [
        sp.block((rows, width), lambda h, g: (g, h), BF16),
        sp.block((t, width), lambda h, g: (0, n_groups + h), BF16),
        sp.block((t, width), lambda h, g: (0, 2 * n_groups + h), BF16),
        sp.block((None, heads, rows, keys), lambda h, g: (layer, h, 0, 0), F32),
    ]
    out_spec = sp.block((rows, width), lambda h, g: (g, h), F32)
    return pl.pallas_call(
        functools.partial(_attn_prompt_kernel, scale=hd ** -0.5, hd=hd, heads=heads),
        grid=(n_groups, t // rows),
        in_specs=in_specs,
        out_specs=out_spec,
        out_shape=jax.ShapeDtypeStruct((t, n_heads * hd), F32),
        compiler_params=sp.params(("parallel", "parallel"), temp_bytes=6 * _nbytes((rows, keys), F32)),
        name="attn_prompt",
    )(qkv, qkv, qkv, bias)


def _attn_sample_kernel(q_ref, kn_ref, vn_ref, kc_ref, vc_ref, bc_ref, bn_ref, o_ref, *, scale, n_heads, hd):
    keep = kc_ref.shape[0] // n_heads
    for h in range(n_heads):
        c = slice(h * hd, (h + 1) * hd)
        frames = pl.ds(h, keep, stride=n_heads)
        q = q_ref[:, c]
        s_cache = _qk(q, kc_ref[frames, :].astype(BF16)) * scale + bc_ref[h]
        s_new = _qk(q, kn_ref[:, c]) * scale + bn_ref[h]
        o_ref[:, c] = _softmax_pv([(s_cache, vc_ref[frames, :].astype(BF16)), (s_new, vn_ref[:, c])])


def _attn_sample(q, kv, cache_k, cache_v, bias_cache, bias_new, layer, n_heads, hd):
    m, da = q.shape
    n_streams = m // CHUNK
    keep = cache_k.shape[2] // n_heads
    sp = _Specs()
    in_specs = [
        sp.block((CHUNK, da), lambda b: (b, 0), BF16),
        sp.block((CHUNK, da), lambda b: (b, 0), BF16),
        sp.block((CHUNK, da), lambda b: (b, 1), BF16),
        sp.block((None, None, keep * n_heads, hd), lambda b: (layer, b, 0, 0), F32),
        sp.block((None, None, keep * n_heads, hd), lambda b: (layer, b, 0, 0), F32),
        sp.block((None, n_heads, CHUNK, keep), lambda b: (layer, 0, 0, 0), F32),
        sp.block((None, n_heads, CHUNK, CHUNK), lambda b: (layer, 0, 0, 0), F32),
    ]
    out_spec = sp.block((CHUNK, da), lambda b: (b, 0), F32)
    return pl.pallas_call(
        functools.partial(_attn_sample_kernel, scale=hd ** -0.5, n_heads=n_heads, hd=hd),
        grid=(n_streams,),
        in_specs=in_specs,
        out_specs=out_spec,
        out_shape=jax.ShapeDtypeStruct((m, da), F32),
        compiler_params=sp.params(("parallel",), temp_bytes=4 * 1024 * 1024),
        name="attn_sample",
    )(q, kv, kv, cache_k, cache_v, bias_cache, bias_new)


def _bias_tables(rel_bias):
    depth, n_heads, n_rel = rel_bias.shape
    max_rel = (n_rel - 1) // 2
    rows, cols = Q_CHUNKS * CHUNK, BAND_CHUNKS * CHUNK
    p = rows + cols - 1
    x = np.arange(p)
    rel = np.where(x < cols, -x, p - x) + N_PAST_CHUNKS * CHUNK
    diag = rel_bias[:, :, np.clip(rel, -max_rel, max_rel) + max_rel]
    a = jnp.tile(diag, (1, 1, rows))[:, :, :rows * (p - 1)].reshape(depth, n_heads, rows, p - 1)
    full = a[..., :cols]
    qi = np.arange(rows)[:, None] // CHUNK
    kj = np.arange(cols)[None, :] // CHUNK
    valid = (kj >= qi) & (kj <= qi + N_PAST_CHUNKS)
    band = jnp.where(valid, full, NEG_INF)
    keep = N_PAST_CHUNKS * CHUNK
    return band, full[:, :, :CHUNK, :keep], full[:, :, :CHUNK, keep:keep + CHUNK]


def _conv_cat_kernel(u_ref, hist_ref, attn_ref, wdw_ref, bdw_ref, gln_ref, bln_ref, gao_ref, gco_ref,
                     cat_ref, xs_ref, y_ref, bc_ref, *, tt, conv_width, hist_is_state):
    da = attn_ref.shape[1]
    dc = u_ref.shape[1]
    if hist_is_state:
        n_hist = hist_ref.shape[0]
        xs_ref[0, 0:V7X_SUBLANES, :] = jnp.zeros((V7X_SUBLANES, dc), F32)
        xs_ref[0, HALO - n_hist:HALO, :] = hist_ref[...]
    else:
        xs_ref[0, 0:HALO, :] = jnp.where(pl.program_id(0) > 0, hist_ref[...], 0.0)
    xs_ref[0, HALO:HALO + tt, :] = u_ref[...]
    first = HALO - (conv_width - 1)
    n_shift = HALO + tt - V7X_SUBLANES
    for r in range(1, V7X_SUBLANES):
        xs_ref[r, 0:n_shift, :] = xs_ref[0, r:r + n_shift, :]
    rc_rows = 64
    for rc in range(tt // rc_rows):
        for cb in range(dc // V7X_LANES):
            c = slice(cb * V7X_LANES, (cb + 1) * V7X_LANES)
            acc = None
            for w in range(conv_width):
                r0 = rc * rc_rows + (first + w) // V7X_SUBLANES * V7X_SUBLANES
                term = xs_ref[(first + w) % V7X_SUBLANES, r0:r0 + rc_rows, c] * wdw_ref[w:w + 1, c]
                acc = term if acc is None else acc + term
            y_ref[rc * rc_rows:(rc + 1) * rc_rows, c] = acc + bdw_ref[:, c]

    for k, ref in enumerate((gln_ref, bln_ref, gco_ref, gao_ref)):
        bc_ref[k] = jnp.broadcast_to(ref[...], bc_ref.shape[1:])

    def strip(j, carry):
        r = pl.ds(pl.multiple_of(j * CONV_STRIP, CONV_STRIP), CONV_STRIP)
        y = _split(y_ref[r, :])
        yc = y - jnp.mean(y, axis=-1, keepdims=True)
        yn = yc * _inv_rms(yc) * bc_ref[0] + bc_ref[1]
        z = yn * _sigmoid(yn)
        cat_ref[r, da:da + dc] = _merge(z * _inv_rms(z) * bc_ref[2]).astype(BF16)
        a = _split(attn_ref[r, :])
        cat_ref[r, 0:da] = _merge(a * _inv_rms(a) * bc_ref[3]).astype(BF16)
        return carry
    n_strips = tt // CONV_STRIP
    lax.fori_loop(0, n_strips, strip, 0, unroll=2 if n_strips > 2 else 1)


def _conv_cat(u, state, attn, w_dw, b_dw, g_ln, b_ln, g_ao, g_co, layer, tt):
    m, dc = u.shape
    da = attn.shape[1]
    conv_width = w_dw.shape[1]
    assert conv_width - 1 <= HALO and tt % HALO == 0
    assert da == dc
    sp = _Specs()
    vec = lambda width: sp.block((None, 1, width), lambda i: (layer, 0, 0), F32)
    if state is None:
        hist, hist_spec = u, sp.block((HALO, dc), lambda i: (jnp.maximum(i * (tt // HALO) - 1, 0), 0), F32)
    else:
        assert state.shape[1:] == (m // tt, conv_width - 1, dc)
        hist, hist_spec = state, sp.block((None, None, conv_width - 1, dc), lambda i: (layer, i, 0, 0), F32)
    in_specs = [
        sp.block((tt, dc), lambda i: (i, 0), F32),
        hist_spec,
        sp.block((tt, da), lambda i: (i, 0), F32),
        sp.block((None, conv_width, dc), lambda i: (layer, 0, 0), F32),
        vec(dc), vec(dc), vec(dc), vec(da), vec(dc),
    ]
    out_spec = sp.block((tt, da + dc), lambda i: (i, 0), BF16)
    scratch = [sp.scratch((V7X_SUBLANES, HALO + tt, dc), F32), sp.scratch((tt, dc), F32),
               sp.scratch((4, V7X_SUBLANES, dc), F32)]
    return pl.pallas_call(
        functools.partial(_conv_cat_kernel, tt=tt, conv_width=conv_width, hist_is_state=state is not None),
        grid=(m // tt,),
        in_specs=in_specs,
        out_specs=out_spec,
        out_shape=jax.ShapeDtypeStruct((m, da + dc), BF16),
        scratch_shapes=scratch,
        compiler_params=sp.params(("parallel",), temp_bytes=4 * _nbytes((tt, dc), F32)),
        name="conv_cat",
    )(u, hist, attn, w_dw, b_dw, g_ln, b_ln, g_ao, g_co)


def _outproj_kernel(cat_ref, w_ref, x_ref, gt_ref, sc_ref, sh_ref, x1_ref, h2_ref, bc_ref, *, tm, nk, per_stream):
    k = pl.program_id(1)

    @pl.when(k == 0)
    def _():
        x1_ref[...] = jnp.dot(cat_ref[...], w_ref[...], preferred_element_type=F32)

    @pl.when(k > 0)
    def _():
        x1_ref[...] += jnp.dot(cat_ref[...], w_ref[...], preferred_element_type=F32)

    @pl.when(k == nk - 1)
    def _():
        def strip(r):
            mixed = _split(x1_ref[r, :])
            x1 = _split(x_ref[r, :]) + mixed * _inv_rms(mixed) * bc_ref[0]
            x1_ref[r, :] = _merge(x1)
            h2_ref[r, :] = _merge(x1 * _inv_rms(x1) * bc_ref[1] + bc_ref[2]).astype(BF16)
        _strip_loops(tm, per_stream, [gt_ref, sc_ref, sh_ref], bc_ref, strip)


def _outproj(cat, w_out, x, mod, layer, per_stream, tm):
    m, dmix = cat.shape
    d = w_out.shape[2]
    tk = TK_OUT
    nk = dmix // tk
    sp = _Specs()
    in_specs = [
        sp.block((tm, tk), lambda i, k: (i, k), BF16),
        sp.block((None, tk, d), lambda i, k: (layer, k, 0), BF16),
        sp.block((tm, d), lambda i, k: (i, 0), F32),
        _mod_block(sp, layer, MOD_GT_M, tm, d, per_stream, 2),
        _mod_block(sp, layer, MOD_SC_F, tm, d, per_stream, 2),
        _mod_block(sp, layer, MOD_SH_F, tm, d, per_stream, 2),
    ]
    out_specs = [
        sp.block((tm, d), lambda i, k: (i, 0), F32),
        sp.block((tm, d), lambda i, k: (i, 0), BF16),
    ]
    return pl.pallas_call(
        functools.partial(_outproj_kernel, tm=tm, nk=nk, per_stream=per_stream),
        grid=(m // tm, nk),
        in_specs=in_specs,
        out_specs=out_specs,
        out_shape=[jax.ShapeDtypeStruct((m, d), F32), jax.ShapeDtypeStruct((m, d), BF16)],
        scratch_shapes=[sp.scratch((3, V7X_SUBLANES, d), F32)],
        compiler_params=sp.params(("parallel", "arbitrary")),
        name="out_proj",
    )(cat, w_out, x, mod, mod, mod)


def _ffn_kernel(h_ref, wg_ref, wu_ref, wd_ref, x1_hbm, gt_ref, x2_hbm, acc_ref, xbuf_ref, bc_ref, in_sem, out_sem,
                *, tm, nf, per_stream):
    i = pl.program_id(0)
    f = pl.program_id(1)
    n_chunks = tm // FFN_CHUNK
    strips = FFN_CHUNK // STRIP

    def chunk_rows(c):
        return pl.ds(pl.multiple_of(i * tm + c * FFN_CHUNK, FFN_CHUNK), FFN_CHUNK)

    def x1_copy(c, slot):
        return pltpu.make_async_copy(x1_hbm.at[chunk_rows(c), :], xbuf_ref.at[slot], in_sem.at[slot])

    def out_copy(c):
        src = acc_ref.at[pl.ds(pl.multiple_of(c * FFN_CHUNK, FFN_CHUNK), FFN_CHUNK), :]
        return pltpu.make_async_copy(src, x2_hbm.at[chunk_rows(c), :], out_sem.at[0])

    @pl.when(f == 0)
    def _():
        acc_ref[...] = jnp.zeros_like(acc_ref)

    @pl.when(f == nf - 1)
    def _():
        x1_copy(0, 0).start()
        x1_copy(1, 1).start()

    h = h_ref[...]
    g = jnp.dot(h, wg_ref[...], preferred_element_type=F32)
    u = jnp.dot(h, wu_ref[...], preferred_element_type=F32)
    a = (g * _sigmoid(g) * u).astype(BF16)
    acc_ref[...] += jnp.dot(a, wd_ref[...], preferred_element_type=F32)

    @pl.when(f == nf - 1)
    def _():
        def gate(row):
            bc_ref[0] = jnp.broadcast_to(gt_ref[pl.ds(row, 1), :], bc_ref.shape[1:])
        if not per_stream:
            gate(0)

        def chunk(c, carry):
            slot = c % 2
            x1_copy(c, slot).wait()
            for t in range(strips):
                if per_stream:
                    gate(c * strips + t)
                r = pl.ds(pl.multiple_of(c * FFN_CHUNK + t * STRIP, STRIP), STRIP)
                y = _split(acc_ref[r, :])
                x1 = _split(xbuf_ref[slot, t * STRIP:(t + 1) * STRIP, :])
                acc_ref[r, :] = _merge(x1 + y * _inv_rms(y) * bc_ref[0])
            out_copy(c).start()

            @pl.when(c + 2 < n_chunks)
            def _():
                x1_copy(c + 2, slot).start()
            return carry
        lax.fori_loop(0, n_chunks, chunk, 0)
        for c in range(n_chunks):
            out_copy(c).wait()


def _ffn(h2, wg, wu, wd, x1, mod, layer, per_stream, tm):
    m, d = h2.shape
    dff = wg.shape[1]
    tf = TF_FFN
    nf = dff // tf
    assert dff % tf == 0 and tm % (2 * FFN_CHUNK) == 0 and FFN_CHUNK % STRIP == 0
    assert STRIP == CHUNK or not per_stream
    sp = _Specs()
    in_specs = [
        sp.block((tm, d), lambda i, f: (i, 0), BF16),
        sp.block((d, tf), lambda i, f: (0, f), BF16),
        sp.block((d, tf), lambda i, f: (0, f), BF16),
        sp.block((tf, d), lambda i, f: (f, 0), BF16),
        pl.BlockSpec(memory_space=pl.ANY),
        _mod_block(sp, layer, MOD_GT_F, tm, d, per_stream, 2),
    ]
    scratch = [
        sp.scratch((tm, d), F32),
        sp.scratch((2, FFN_CHUNK, d), F32),
        sp.scratch((1, V7X_SUBLANES, d), F32),
        pltpu.SemaphoreType.DMA((2,)),
        pltpu.SemaphoreType.DMA((1,)),
    ]
    return pl.pallas_call(
        functools.partial(_ffn_kernel, tm=tm, nf=nf, per_stream=per_stream),
        grid=(m // tm, nf),
        in_specs=in_specs,
        out_specs=pl.BlockSpec(memory_space=pl.ANY),
        out_shape=jax.ShapeDtypeStruct((m, d), F32),
        scratch_shapes=scratch,
        compiler_params=sp.params(("arbitrary", "arbitrary"), temp_bytes=4 * _nbytes((tm, tf), F32)),
        name="ffn",
    )(h2, wg, wu, wd, x1, mod)


def kernel(x_prompt, x_sample, cache_k, cache_v, state_conv, c_prompt, c_sample, w_ada, b_ada, g_mix_pre,
           g_mix_post, w_in, rel_bias, w_dw, b_dw, g_conv_ln, b_conv_ln, g_attn_out, g_conv_out, w_out,
           g_ffn_pre, g_ffn_post, w_ffn_gate, w_ffn_up, w_ffn_down):
    batch, seq, d = x_prompt.shape
    dec_batch, dec_seq, _ = x_sample.shape
    depth, _, keep, n_heads, hd = cache_k.shape
    da = n_heads * hd
    dc = state_conv.shape[3]
    conv_state = state_conv.shape[2]
    assert batch == 1 and dec_seq == CHUNK and keep == N_PAST_CHUNKS * CHUNK and dec_batch == PROMPT_MOD_ROW
    assert seq % (Q_CHUNKS * CHUNK) == 0 and seq >= keep
    m_s = dec_batch * dec_seq

    vec = lambda a: a.reshape(depth, 1, a.shape[-1])
    g_mix_pre, g_mix_post, g_ffn_pre, g_ffn_post = map(vec, (g_mix_pre, g_mix_post, g_ffn_pre, g_ffn_post))
    b_dw, g_conv_ln, b_conv_ln, g_attn_out, g_conv_out = map(vec, (b_dw, g_conv_ln, b_conv_ln, g_attn_out, g_conv_out))

    c_all = jnp.concatenate(
        [c_sample, c_prompt, jnp.zeros((MOD_ROWS - dec_batch - batch, d), F32)], axis=0)
    mod = _ada(c_all, w_ada, b_ada, g_mix_pre, g_mix_post, g_ffn_pre, g_ffn_post)

    w_out_b = w_out.astype(BF16)
    dff = w_ffn_gate.shape[2]
    band_bias, cache_bias, new_bias = _bias_tables(rel_bias)
    cache_k2 = cache_k.reshape(depth, dec_batch, keep * n_heads, hd)
    cache_v2 = cache_v.reshape(depth, dec_batch, keep * n_heads, hd)

    xp = x_prompt.reshape(seq, d)
    xs = x_sample.reshape(m_s, d)
    outs = {k: [] for k in ("conv_p", "k_p", "v_p", "conv_s", "k_s", "v_s")}
    tt_p = TT_CONV_PROMPT
    for l in range(depth):
        h = _prenorm(xp, mod, l, per_stream=False)
        qkv, wd_b = _proj(h, w_in, l, row0=0, m=seq, col0=0, n=3 * da, emit_bf16=True, emit_f32=False, tm=TM,
                          side=[_SideConvert(w_ffn_down, l, (dff // SIDE_ROW_BLOCKS, d // SIDE_COL_BLOCKS))])
        (kv_tail,) = _proj(h, w_in, l, row0=seq - keep, m=keep, col0=da, n=2 * da,
                           emit_bf16=False, emit_f32=True, tm=keep)
        u, wg_b, wu_b = _glu(h, w_in, l, col_a=3 * da, col_b=3 * da + dc, n=dc, tm=TM,
                             side=[_SideConvert(w_ffn_gate, l, (d // SIDE_BLOCKS, dff)),
                                   _SideConvert(w_ffn_up, l, (d // SIDE_BLOCKS, dff))])
        attn = _attn_prompt(qkv, band_bias, l, n_heads, hd)
        cat = _conv_cat(u, None, attn, w_dw, b_dw, g_conv_ln, b_conv_ln, g_attn_out, g_conv_out, l, tt_p)
        x1, h2 = _outproj(cat, w_out_b, xp, mod, l, False, TM_OUT)
        xp = _ffn(h2, wg_b, wu_b, wd_b, x1, mod, l, False, TM)
        outs["conv_p"].append(u[seq - conv_state:].reshape(batch, conv_state, dc))
        outs["k_p"].append(kv_tail[:, :da].reshape(batch, keep, n_heads, hd))
        outs["v_p"].append(kv_tail[:, da:].reshape(batch, keep, n_heads, hd))

        h = _prenorm(xs, mod, l, per_stream=True)
        (q_s,) = _proj(h, w_in, l, row0=0, m=m_s, col0=0, n=da, emit_bf16=True, emit_f32=False, tm=TM)
        kv_s, kv_s32 = _proj(h, w_in, l, row0=0, m=m_s, col0=da, n=2 * da, emit_bf16=True, emit_f32=True, tm=TM)
        (u,) = _glu(h, w_in, l, col_a=3 * da, col_b=3 * da + dc, n=dc, tm=TM)
        attn = _attn_sample(q_s, kv_s, cache_k2, cache_v2, cache_bias, new_bias, l, n_heads, hd)
        cat = _conv_cat(u, state_conv, attn, w_dw, b_dw, g_conv_ln, b_conv_ln, g_attn_out, g_conv_out, l, CHUNK)
        x1, h2 = _outproj(cat, w_out_b, xs, mod, l, True, TM_OUT)
        xs = _ffn(h2, wg_b, wu_b, wd_b, x1, mod, l, True, TM)
        outs["conv_s"].append(u.reshape(dec_batch, dec_seq, dc)[:, dec_seq - conv_state:, :])
        outs["k_s"].append(kv_s32[:, :da].reshape(dec_batch, dec_seq, n_heads, hd))
        outs["v_s"].append(kv_s32[:, da:].reshape(dec_batch, dec_seq, n_heads, hd))

    stack = lambda name: jnp.stack(outs[name])
    return (xp.reshape(batch, seq, d), xs.reshape(dec_batch, dec_seq, d),
            stack("conv_p"), stack("k_p"), stack("v_p"), stack("conv_s"), stack("k_s"), stack("v_s"))
```

```python
import functools

import numpy as np
import jax
import jax.numpy as jnp
from jax import lax
from jax.experimental import pallas as pl
from jax.experimental.pallas import tpu as pltpu

F32 = jnp.float32
BF16 = jnp.bfloat16

EPS = 1e-6
NEG_INF = -1e30
CHUNK = 64
N_PAST_CHUNKS = 8
Q_CHUNKS = 4
BAND_CHUNKS = Q_CHUNKS + N_PAST_CHUNKS
N_MOD = 6
MOD_SH_M, MOD_SC_M, MOD_GT_M, MOD_SH_F, MOD_SC_F, MOD_GT_F = range(N_MOD)

V7X_VMEM_BYTES = 64 * 1024 * 1024
V7X_VMEM_REQUEST_CAP = 60 * 1024 * 1024
V7X_SPILL_ALLOWANCE = 4 * 1024 * 1024
V7X_LANES = 128
V7X_SUBLANES = 8
V7X_BF16_SUBLANES = 16

STRIP = 4 * V7X_BF16_SUBLANES
CONV_STRIP = 2 * V7X_BF16_SUBLANES
HALO = 32
NORM_UNROLL = 1

TM = 1024
TN_PROJ = 512
TN_GLU = 256
TM_OUT = 512
TK_OUT = 512
TF_FFN = 256
FFN_CHUNK = 128
TM_EW = 512
TN_ADA = 512
TT_CONV_PROMPT = 128
SIDE_BLOCKS = 64
SIDE_ROW_BLOCKS, SIDE_COL_BLOCKS = 16, 4
ATTN_HEADS_PER_STEP = 4
MOD_ROWS = 48
PROMPT_MOD_ROW = 32


def _sigmoid(x):
    return 1.0 / (1.0 + jnp.exp(-x))


def _nbytes(shape, dtype):
    return int(np.prod(shape)) * jnp.dtype(dtype).itemsize


class _Specs:
    def __init__(self):
        self.bytes = 0

    def block(self, shape, index_map, dtype, buffers=2):
        real = tuple(s for s in shape if s is not None)
        self.bytes += buffers * _nbytes(real, dtype)
        if buffers == 1:
            return pl.BlockSpec(shape, index_map, pipeline_mode=pl.Buffered(1))
        return pl.BlockSpec(shape, index_map)

    def scratch(self, shape, dtype):
        self.bytes += _nbytes(shape, dtype)
        return pltpu.VMEM(shape, dtype)

    def params(self, semantics, temp_bytes=0):
        limit = min(self.bytes + temp_bytes + V7X_SPILL_ALLOWANCE, V7X_VMEM_REQUEST_CAP)
        return pltpu.CompilerParams(dimension_semantics=semantics, vmem_limit_bytes=limit)


def _ada_kernel(c_ref, w_ref, b_ref, gmpre_ref, gmpost_ref, gfpre_ref, gfpost_ref, o_ref, *, blocks_per_mod):
    which = pl.program_id(1) // blocks_per_mod
    c = c_ref[...]
    a = (c * _sigmoid(c)).astype(BF16)
    w = w_ref[...].astype(BF16)
    m = jnp.dot(a, w, preferred_element_type=F32) + b_ref[...]
    g_scale = jnp.where(which == MOD_SC_M, gmpre_ref[...], gfpre_ref[...])
    g_gate = jnp.where(which == MOD_GT_M, gmpost_ref[...], gfpost_ref[...])
    is_scale = jnp.logical_or(which == MOD_SC_M, which == MOD_SC_F)
    is_gate = jnp.logical_or(which == MOD_GT_M, which == MOD_GT_F)
    o_ref[...] = jnp.where(is_scale, g_scale * (1.0 + m), jnp.where(is_gate, g_gate * m, m))


def _ada(c_all, w_ada, b_ada, g_mix_pre, g_mix_post, g_ffn_pre, g_ffn_post):
    depth, d, n = w_ada.shape
    rows = c_all.shape[0]
    blocks_per_mod = d // TN_ADA
    sp = _Specs()
    gain = lambda: sp.block((None, 1, TN_ADA), lambda l, j: (l, 0, j % blocks_per_mod), F32)
    in_specs = [
        sp.block((rows, d), lambda l, j: (0, 0), F32),
        sp.block((None, d, TN_ADA), lambda l, j: (l, 0, j), F32),
        sp.block((None, 1, TN_ADA), lambda l, j: (l, 0, j), F32),
        gain(), gain(), gain(), gain(),
    ]
    out_spec = sp.block((None, rows, TN_ADA), lambda l, j: (l, 0, j), F32)
    return pl.pallas_call(
        functools.partial(_ada_kernel, blocks_per_mod=blocks_per_mod),
        grid=(depth, n // TN_ADA),
        in_specs=in_specs,
        out_specs=out_spec,
        out_shape=jax.ShapeDtypeStruct((depth, rows, n), F32),
        compiler_params=sp.params(("parallel", "parallel"), temp_bytes=_nbytes((d, TN_ADA), BF16)),
        name="ada_modulation",
    )(c_all, w_ada, b_ada.reshape(depth, 1, n), g_mix_pre, g_mix_post, g_ffn_pre, g_ffn_post)


def _mod_block(sp, layer, which, tm, d, per_stream, grid_rank):
    if per_stream:
        rows = tm // CHUNK
        assert rows % 8 == 0
        if grid_rank == 1:
            imap = lambda i: (layer, i, which)
        else:
            imap = lambda i, j: (layer, i, which)
    else:
        rows = 8
        blk = PROMPT_MOD_ROW // 8
        if grid_rank == 1:
            imap = lambda i: (layer, blk, which)
        else:
            imap = lambda i, j: (layer, blk, which)
    return sp.block((None, rows, d), imap, F32)


def _strip_loops(tm, per_stream, vec_refs, bc_ref, strip_fn):
    rows_per_vec = CHUNK if per_stream else tm
    n_inner = rows_per_vec // STRIP

    def outer(s, carry):
        for k, ref in enumerate(vec_refs):
            bc_ref[k] = jnp.broadcast_to(ref[pl.ds(s, 1), :], bc_ref.shape[1:])

        def inner(j, c2):
            strip_fn(pl.ds(pl.multiple_of(s * rows_per_vec + j * STRIP, STRIP), STRIP))
            return c2
        lax.fori_loop(0, n_inner, inner, 0, unroll=min(n_inner, NORM_UNROLL))
        return carry
    lax.fori_loop(0, tm // rows_per_vec, outer, 0)


def _split(x):
    return x.reshape(x.shape[0] // V7X_SUBLANES, V7X_SUBLANES, x.shape[-1])


def _merge(x3):
    return x3.reshape(x3.shape[0] * V7X_SUBLANES, x3.shape[-1])


def _inv_rms(x):
    return lax.rsqrt(jnp.mean(x * x, axis=-1, keepdims=True) + EPS)


def _prenorm_kernel(x_ref, sc_ref, sh_ref, h_ref, bc_ref, *, tm, per_stream):
    def strip(r):
        x = _split(x_ref[r, :])
        h_ref[r, :] = _merge(x * _inv_rms(x) * bc_ref[0] + bc_ref[1]).astype(BF16)
    _strip_loops(tm, per_stream, [sc_ref, sh_ref], bc_ref, strip)


def _prenorm(x, mod, layer, per_stream):
    m, d = x.shape
    tm = TM_EW
    sp = _Specs()
    in_specs = [
        sp.block((tm, d), lambda i: (i, 0), F32),
        _mod_block(sp, layer, MOD_SC_M, tm, d, per_stream, 1),
        _mod_block(sp, layer, MOD_SH_M, tm, d, per_stream, 1),
    ]
    out_spec = sp.block((tm, d), lambda i: (i, 0), BF16)
    return pl.pallas_call(
        functools.partial(_prenorm_kernel, tm=tm, per_stream=per_stream),
        grid=(m // tm,),
        in_specs=in_specs,
        out_specs=out_spec,
        out_shape=jax.ShapeDtypeStruct((m, d), BF16),
        scratch_shapes=[sp.scratch((2, V7X_SUBLANES, d), F32)],
        compiler_params=sp.params(("parallel",)),
        name="prenorm",
    )(x, mod, mod)


class _SideConvert:
    def __init__(self, src, layer, block):
        self.src, self.layer, self.block = src, layer, block
        rows, cols = src.shape[1:]
        br, bc = block
        assert rows % br == 0 and cols % bc == 0 and br % V7X_BF16_SUBLANES == 0 and bc % V7X_LANES == 0
        self.per_row = cols // bc
        self.n_blocks = (rows // br) * self.per_row

    def specs(self, sp, grid):
        n_steps = int(np.prod(grid))
        assert n_steps >= self.n_blocks
        last, per_row, layer = self.n_blocks - 1, self.per_row, self.layer

        def block_of(*idx):
            step = idx[0]
            for size, k in zip(grid[1:], idx[1:]):
                step = step * size + k
            b = jnp.minimum(step, last)
            return b // per_row, b % per_row
        in_spec = sp.block((None,) + self.block, lambda *idx: (layer,) + block_of(*idx), self.src.dtype)
        out_spec = sp.block(self.block, block_of, BF16)
        return in_spec, out_spec, jax.ShapeDtypeStruct(self.src.shape[1:], BF16)


def _side_specs(sp, side, grid):
    specs = [c.specs(sp, grid) for c in side]
    return [s[0] for s in specs], [s[1] for s in specs], [s[2] for s in specs], [c.src for c in side]


def _run_side_converts(refs, n_side):
    for src_ref, dst_ref in zip(refs[:n_side], refs[len(refs) - n_side:]):
        dst_ref[...] = src_ref[...].astype(BF16)


def _proj_kernel(h_ref, w_ref, *refs, emit_bf16, emit_f32, n_side):
    acc = jnp.dot(h_ref[...], w_ref[...].astype(BF16), preferred_element_type=F32)
    outs = list(refs[n_side:])
    if emit_bf16:
        outs.pop(0)[...] = acc.astype(BF16)
    if emit_f32:
        outs.pop(0)[...] = acc
    _run_side_converts(refs, n_side)


def _proj(h, w, layer, *, row0, m, col0, n, emit_bf16, emit_f32, tm, side=()):
    d = h.shape[1]
    tn = TN_PROJ
    assert row0 % tm == 0 and m % tm == 0 and col0 % tn == 0 and n % tn == 0
    rb, cb = row0 // tm, col0 // tn
    grid = (m // tm, n // tn)
    sp = _Specs()
    side_in, side_out, side_shape, side_args = _side_specs(sp, side, grid)
    in_specs = [
        sp.block((tm, d), lambda i, j: (i + rb, 0), BF16),
        sp.block((None, d, tn), lambda i, j: (layer, 0, j + cb), w.dtype),
    ] + side_in
    out_specs, out_shape = [], []
    if emit_bf16:
        out_specs.append(sp.block((tm, tn), lambda i, j: (i, j), BF16))
        out_shape.append(jax.ShapeDtypeStruct((m, n), BF16))
    if emit_f32:
        out_specs.append(sp.block((tm, tn), lambda i, j: (i, j), F32))
        out_shape.append(jax.ShapeDtypeStruct((m, n), F32))
    return pl.pallas_call(
        functools.partial(_proj_kernel, emit_bf16=emit_bf16, emit_f32=emit_f32, n_side=len(side)),
        grid=grid,
        in_specs=in_specs,
        out_specs=out_specs + side_out,
        out_shape=out_shape + side_shape,
        compiler_params=sp.params(("arbitrary", "arbitrary"),
                                  temp_bytes=_nbytes((tm, tn), F32) + _nbytes((d, tn), BF16)),
        name="in_proj",
    )(h, w, *side_args)


def _glu_kernel(h_ref, wa_ref, wb_ref, *refs, n_side):
    h = h_ref[...]
    a = jnp.dot(h, wa_ref[...].astype(BF16), preferred_element_type=F32)
    b = jnp.dot(h, wb_ref[...].astype(BF16), preferred_element_type=F32)
    refs[n_side][...] = a * _sigmoid(b)
    _run_side_converts(refs, n_side)


def _glu(h, w, layer, *, col_a, col_b, n, tm, side=()):
    m, d = h.shape
    tn = TN_GLU
    ca, cb = col_a // tn, col_b // tn
    grid = (m // tm, n // tn)
    sp = _Specs()
    side_in, side_out, side_shape, side_args = _side_specs(sp, side, grid)
    in_specs = [
        sp.block((tm, d), lambda i, j: (i, 0), BF16),
        sp.block((None, d, tn), lambda i, j: (layer, 0, j + ca), w.dtype),
        sp.block((None, d, tn), lambda i, j: (layer, 0, j + cb), w.dtype),
    ] + side_in
    out_spec = sp.block((tm, tn), lambda i, j: (i, j), F32)
    return pl.pallas_call(
        functools.partial(_glu_kernel, n_side=len(side)),
        grid=grid,
        in_specs=in_specs,
        out_specs=[out_spec] + side_out,
        out_shape=[jax.ShapeDtypeStruct((m, n), F32)] + side_shape,
        compiler_params=sp.params(("arbitrary", "arbitrary"),
                                  temp_bytes=3 * _nbytes((tm, tn), F32) + 2 * _nbytes((d, tn), BF16)),
        name="in_proj_glu",
    )(h, w, w, *side_args)


def _softmax_pv(parts):
    m = None
    for s, _ in parts:
        mi = jnp.max(s, axis=-1, keepdims=True)
        m = mi if m is None else jnp.maximum(m, mi)
    l, o = None, None
    for s, v in parts:
        p = jnp.exp(s - m)
        li = jnp.sum(p, axis=-1, keepdims=True)
        oi = jnp.dot(p.astype(BF16), v, preferred_element_type=F32)
        l = li if l is None else l + li
        o = oi if o is None else o + oi
    return o / l


def _qk(q, k):
    return lax.dot_general(q, k, (((1,), (1,)), ((), ())), preferred_element_type=F32)


def _attn_prompt_kernel(q_ref, k_ref, v_ref, b_ref, *refs, scale, hd, heads, n_side):
    g = pl.program_id(1)
    rows = Q_CHUNKS * CHUNK
    o_ref = refs[n_side]
    _run_side_converts(refs, n_side)

    def run(k0, n_keys, b_off):
        for h in range(heads):
            c = slice(h * hd, (h + 1) * hd)
            k = k_ref[pl.ds(k0, n_keys), c]
            v = v_ref[pl.ds(k0, n_keys), c]
            s = _qk(q_ref[:, c], k) * scale + b_ref[h, :, b_off:b_off + n_keys]
            o_ref[:, c] = _softmax_pv([(s, v)])

    n_lead = N_PAST_CHUNKS // Q_CHUNKS
    for lead in range(n_lead):
        keys = (lead + 1) * rows
        pl.when(g == lead)(functools.partial(run, 0, keys, BAND_CHUNKS * CHUNK - keys))

    @pl.when(g >= n_lead)
    def _():
        run(pl.multiple_of((g - n_lead) * rows, rows), BAND_CHUNKS * CHUNK, 0)


def _attn_prompt(qkv, bias, layer, n_heads, hd, side=()):
    t = qkv.shape[0]
    rows = Q_CHUNKS * CHUNK
    keys = BAND_CHUNKS * CHUNK
    heads = ATTN_HEADS_PER_STEP
    assert n_heads % heads == 0
    n_groups = n_heads // heads
    width = heads * hd
    grid = (n_groups, t // rows)
    sp = _Specs()
    side_in, side_out, side_shape, side_args = _side_specs(sp, side, grid)
    in_specs = [
        sp.block((rows, width), lambda h, g: (g, h), BF16),
        sp.block((t, width), lambda h, g: (0, n_groups + h), BF16),
        sp.block((t, width), lambda h, g: (0, 2 * n_groups + h), BF16),
        sp.block((None, heads, rows, keys), lambda h, g: (layer, h, 0, 0), F32),
    ] + side_in
    out_spec = sp.block((rows, width), lambda h, g: (g, h), F32)
    return pl.pallas_call(
        functools.partial(_attn_prompt_kernel, scale=hd ** -0.5, hd=hd, heads=heads, n_side=len(side)),
        grid=grid,
        in_specs=in_specs,
        out_specs=[out_spec] + side_out,
        out_shape=[jax.ShapeDtypeStruct((t, n_heads * hd), F32)] + side_shape,
        compiler_params=sp.params(("arbitrary", "arbitrary"), temp_bytes=6 * _nbytes((rows, keys), F32)),
        name="attn_prompt",
    )(qkv, qkv, qkv, bias, *side_args)


def _attn_sample_kernel(q_ref, kn_ref, vn_ref, kc_ref, vc_ref, bc_ref, bn_ref, o_ref, *, scale, n_heads, hd):
    keep = kc_ref.shape[0] // n_heads
    for h in range(n_heads):
        c = slice(h * hd, (h + 1) * hd)
        frames = pl.ds(h, keep, stride=n_heads)
        q = q_ref[:, c]
        s_cache = _qk(q, kc_ref[frames, :].astype(BF16)) * scale + bc_ref[h]
        s_new = _qk(q, kn_ref[:, c]) * scale + bn_ref[h]
        o_ref[:, c] = _softmax_pv([(s_cache, vc_ref[frames, :].astype(BF16)), (s_new, vn_ref[:, c])])


def _attn_sample(q, kv, cache_k, cache_v, bias_cache, bias_new, layer, n_heads, hd):
    m, da = q.shape
    n_streams = m // CHUNK
    keep = cache_k.shape[2] // n_heads
    sp = _Specs()
    in_specs = [
        sp.block((CHUNK, da), lambda b: (b, 0), BF16),
        sp.block((CHUNK, da), lambda b: (b, 0), BF16),
        sp.block((CHUNK, da), lambda b: (b, 1), BF16),
        sp.block((None, None, keep * n_heads, hd), lambda b: (layer, b, 0, 0), F32),
        sp.block((None, None, keep * n_heads, hd), lambda b: (layer, b, 0, 0), F32),
        sp.block((None, n_heads, CHUNK, keep), lambda b: (layer, 0, 0, 0), F32),
        sp.block((None, n_heads, CHUNK, CHUNK), lambda b: (layer, 0, 0, 0), F32),
    ]
    out_spec = sp.block((CHUNK, da), lambda b: (b, 0), F32)
    return pl.pallas_call(
        functools.partial(_attn_sample_kernel, scale=hd ** -0.5, n_heads=n_heads, hd=hd),
        grid=(n_streams,),
        in_specs=in_specs,
        out_specs=out_spec,
        out_shape=jax.ShapeDtypeStruct((m, da), F32),
        compiler_params=sp.params(("parallel",), temp_bytes=4 * 1024 * 1024),
        name="attn_sample",
    )(q, kv, kv, cache_k, cache_v, bias_cache, bias_new)


def _bias_tables(rel_bias):
    depth, n_heads, n_rel = rel_bias.shape
    max_rel = (n_rel - 1) // 2
    keep = N_PAST_CHUNKS * CHUNK
    lead = (Q_CHUNKS - 1) * CHUNK
    cols = BAND_CHUNKS * CHUNK
    width = lead + cols
    p = CHUNK + width - 1
    x = np.arange(p)
    rel = np.where(x < width, -x, p - x) + keep + lead
    diag = rel_bias[:, :, np.clip(rel, -max_rel, max_rel) + max_rel]
    a = jnp.tile(diag, (1, 1, CHUNK))[:, :, :CHUNK * (p - 1)].reshape(depth, n_heads, CHUNK, p - 1)
    strip = a[..., :width]
    col = np.arange(width)
    masked = jnp.where((col >= lead) & (col < lead + keep + CHUNK), strip, NEG_INF)
    band = jnp.concatenate([masked[..., lead - qi * CHUNK:lead - qi * CHUNK + cols] for qi in range(Q_CHUNKS)], axis=2)
    return band, strip[..., lead:lead + keep], strip[..., lead + keep:lead + keep + CHUNK]


def _conv_cat_kernel(u_ref, hist_ref, attn_ref, wdw_ref, bdw_ref, gln_ref, bln_ref, gao_ref, gco_ref,
                     *refs, tt, conv_width, hist_is_state, n_side):
    xs_ref, y_ref, bc_ref = refs[-3:]
    cat_ref = refs[n_side]
    _run_side_converts(refs[:-3], n_side)
    da = attn_ref.shape[1]
    dc = u_ref.shape[1]
    if hist_is_state:
        n_hist = hist_ref.shape[0]
        xs_ref[0, 0:V7X_SUBLANES, :] = jnp.zeros((V7X_SUBLANES, dc), F32)
        xs_ref[0, HALO - n_hist:HALO, :] = hist_ref[...]
    else:
        xs_ref[0, 0:HALO, :] = jnp.where(pl.program_id(0) > 0, hist_ref[...], 0.0)
    xs_ref[0, HALO:HALO + tt, :] = u_ref[...]
    first = HALO - (conv_width - 1)
    n_shift = HALO + tt - V7X_SUBLANES
    for r in range(1, V7X_SUBLANES):
        xs_ref[r, 0:n_shift, :] = xs_ref[0, r:r + n_shift, :]
    rc_rows = 64
    for rc in range(tt // rc_rows):
        for cb in range(dc // V7X_LANES):
            c = slice(cb * V7X_LANES, (cb + 1) * V7X_LANES)
            acc = None
            for w in range(conv_width):
                r0 = rc * rc_rows + (first + w) // V7X_SUBLANES * V7X_SUBLANES
                term = xs_ref[(first + w) % V7X_SUBLANES, r0:r0 + rc_rows, c] * wdw_ref[w:w + 1, c]
                acc = term if acc is None else acc + term
            y_ref[rc * rc_rows:(rc + 1) * rc_rows, c] = acc + bdw_ref[:, c]

    for k, ref in enumerate((gln_ref, bln_ref, gco_ref, gao_ref)):
        bc_ref[k] = jnp.broadcast_to(ref[...], bc_ref.shape[1:])

    def strip(j, carry):
        r = pl.ds(pl.multiple_of(j * CONV_STRIP, CONV_STRIP), CONV_STRIP)
        y = _split(y_ref[r, :])
        yc = y - jnp.mean(y, axis=-1, keepdims=True)
        yn = yc * _inv_rms(yc) * bc_ref[0] + bc_ref[1]
        z = yn * _sigmoid(yn)
        cat_ref[r, da:da + dc] = _merge(z * _inv_rms(z) * bc_ref[2]).astype(BF16)
        a = _split(attn_ref[r, :])
        cat_ref[r, 0:da] = _merge(a * _inv_rms(a) * bc_ref[3]).astype(BF16)
        return carry
    n_strips = tt // CONV_STRIP
    lax.fori_loop(0, n_strips, strip, 0, unroll=2 if n_strips > 2 else 1)


def _conv_cat(u, state, attn, w_dw, b_dw, g_ln, b_ln, g_ao, g_co, layer, tt, side=()):
    m, dc = u.shape
    da = attn.shape[1]
    conv_width = w_dw.shape[1]
    assert conv_width - 1 <= HALO and tt % HALO == 0
    assert da == dc
    grid = (m // tt,)
    sp = _Specs()
    side_in, side_out, side_shape, side_args = _side_specs(sp, side, grid)
    vec = lambda width: sp.block((None, 1, width), lambda i: (layer, 0, 0), F32)
    if state is None:
        hist, hist_spec = u, sp.block((HALO, dc), lambda i: (jnp.maximum(i * (tt // HALO) - 1, 0), 0), F32)
    else:
        assert state.shape[1:] == (m // tt, conv_width - 1, dc)
        hist, hist_spec = state, sp.block((None, None, conv_width - 1, dc), lambda i: (layer, i, 0, 0), F32)
    in_specs = [
        sp.block((tt, dc), lambda i: (i, 0), F32),
        hist_spec,
        sp.block((tt, da), lambda i: (i, 0), F32),
        sp.block((None, conv_width, dc), lambda i: (layer, 0, 0), F32),
        vec(dc), vec(dc), vec(dc), vec(da), vec(dc),
    ] + side_in
    out_spec = sp.block((tt, da + dc), lambda i: (i, 0), BF16)
    scratch = [sp.scratch((V7X_SUBLANES, HALO + tt, dc), F32), sp.scratch((tt, dc), F32),
               sp.scratch((4, V7X_SUBLANES, dc), F32)]
    return pl.pallas_call(
        functools.partial(_conv_cat_kernel, tt=tt, conv_width=conv_width, hist_is_state=state is not None,
                          n_side=len(side)),
        grid=grid,
        in_specs=in_specs,
        out_specs=[out_spec] + side_out,
        out_shape=[jax.ShapeDtypeStruct((m, da + dc), BF16)] + side_shape,
        scratch_shapes=scratch,
        compiler_params=sp.params(("arbitrary",), temp_bytes=4 * _nbytes((tt, dc), F32)),
        name="conv_cat",
    )(u, hist, attn, w_dw, b_dw, g_ln, b_ln, g_ao, g_co, *side_args)


def _outproj_kernel(cat_ref, w_ref, x_ref, gt_ref, sc_ref, sh_ref, x1_ref, h2_ref, bc_ref, *, tm, nk, per_stream):
    k = pl.program_id(1)

    @pl.when(k == 0)
    def _():
        x1_ref[...] = jnp.dot(cat_ref[...], w_ref[...], preferred_element_type=F32)

    @pl.when(k > 0)
    def _():
        x1_ref[...] += jnp.dot(cat_ref[...], w_ref[...], preferred_element_type=F32)

    @pl.when(k == nk - 1)
    def _():
        def strip(r):
            mixed = _split(x1_ref[r, :])
            x1 = _split(x_ref[r, :]) + mixed * _inv_rms(mixed) * bc_ref[0]
            x1_ref[r, :] = _merge(x1)
            h2_ref[r, :] = _merge(x1 * _inv_rms(x1) * bc_ref[1] + bc_ref[2]).astype(BF16)
        _strip_loops(tm, per_stream, [gt_ref, sc_ref, sh_ref], bc_ref, strip)


def _outproj(cat, w_out, x, mod, layer, per_stream, tm):
    m, dmix = cat.shape
    d = w_out.shape[1]
    tk = TK_OUT
    nk = dmix // tk
    sp = _Specs()
    in_specs = [
        sp.block((tm, tk), lambda i, k: (i, k), BF16),
        sp.block((tk, d), lambda i, k: (k, 0), BF16),
        sp.block((tm, d), lambda i, k: (i, 0), F32),
        _mod_block(sp, layer, MOD_GT_M, tm, d, per_stream, 2),
        _mod_block(sp, layer, MOD_SC_F, tm, d, per_stream, 2),
        _mod_block(sp, layer, MOD_SH_F, tm, d, per_stream, 2),
    ]
    out_specs = [
        sp.block((tm, d), lambda i, k: (i, 0), F32),
        sp.block((tm, d), lambda i, k: (i, 0), BF16),
    ]
    return pl.pallas_call(
        functools.partial(_outproj_kernel, tm=tm, nk=nk, per_stream=per_stream),
        grid=(m // tm, nk),
        in_specs=in_specs,
        out_specs=out_specs,
        out_shape=[jax.ShapeDtypeStruct((m, d), F32), jax.ShapeDtypeStruct((m, d), BF16)],
        scratch_shapes=[sp.scratch((3, V7X_SUBLANES, d), F32)],
        compiler_params=sp.params(("parallel", "arbitrary")),
        name="out_proj",
    )(cat, w_out, x, mod, mod, mod)


def _ffn_kernel(h_ref, wg_ref, wu_ref, wd_ref, x1_hbm, gt_ref, x2_hbm, acc_ref, xbuf_ref, bc_ref, in_sem, out_sem,
                *, tm, nf, per_stream):
    i = pl.program_id(0)
    f = pl.program_id(1)
    n_chunks = tm // FFN_CHUNK
    strips = FFN_CHUNK // STRIP

    def chunk_rows(c):
        return pl.ds(pl.multiple_of(i * tm + c * FFN_CHUNK, FFN_CHUNK), FFN_CHUNK)

    def x1_copy(c, slot):
        return pltpu.make_async_copy(x1_hbm.at[chunk_rows(c), :], xbuf_ref.at[slot], in_sem.at[slot])

    def out_copy(c):
        src = acc_ref.at[pl.ds(pl.multiple_of(c * FFN_CHUNK, FFN_CHUNK), FFN_CHUNK), :]
        return pltpu.make_async_copy(src, x2_hbm.at[chunk_rows(c), :], out_sem.at[0])

    @pl.when(f == 0)
    def _():
        acc_ref[...] = jnp.zeros_like(acc_ref)

    @pl.when(f == nf - 1)
    def _():
        x1_copy(0, 0).start()
        x1_copy(1, 1).start()

    h = h_ref[...]
    g = jnp.dot(h, wg_ref[...], preferred_element_type=F32)
    u = jnp.dot(h, wu_ref[...], preferred_element_type=F32)
    a = (g * _sigmoid(g) * u).astype(BF16)
    acc_ref[...] += jnp.dot(a, wd_ref[...], preferred_element_type=F32)

    @pl.when(f == nf - 1)
    def _():
        def gate(row):
            bc_ref[0] = jnp.broadcast_to(gt_ref[pl.ds(row, 1), :], bc_ref.shape[1:])
        if not per_stream:
            gate(0)

        def chunk(c, carry):
            slot = c % 2
            x1_copy(c, slot).wait()
            for t in range(strips):
                if per_stream:
                    gate(c * strips + t)
                r = pl.ds(pl.multiple_of(c * FFN_CHUNK + t * STRIP, STRIP), STRIP)
                y = _split(acc_ref[r, :])
                x1 = _split(xbuf_ref[slot, t * STRIP:(t + 1) * STRIP, :])
                acc_ref[r, :] = _merge(x1 + y * _inv_rms(y) * bc_ref[0])
            out_copy(c).start()

            @pl.when(c + 2 < n_chunks)
            def _():
                x1_copy(c + 2, slot).start()
            return carry
        lax.fori_loop(0, n_chunks, chunk, 0)
        for c in range(n_chunks):
            out_copy(c).wait()


def _ffn(h2, wg, wu, wd, x1, mod, layer, per_stream, tm):
    m, d = h2.shape
    dff = wg.shape[1]
    tf = TF_FFN
    nf = dff // tf
    assert dff % tf == 0 and tm % (2 * FFN_CHUNK) == 0 and FFN_CHUNK % STRIP == 0
    assert STRIP == CHUNK or not per_stream
    sp = _Specs()
    in_specs = [
        sp.block((tm, d), lambda i, f: (i, 0), BF16),
        sp.block((d, tf), lambda i, f: (0, f), BF16),
        sp.block((d, tf), lambda i, f: (0, f), BF16),
        sp.block((tf, d), lambda i, f: (f, 0), BF16),
        pl.BlockSpec(memory_space=pl.ANY),
        _mod_block(sp, layer, MOD_GT_F, tm, d, per_stream, 2),
    ]
    scratch = [
        sp.scratch((tm, d), F32),
        sp.scratch((2, FFN_CHUNK, d), F32),
        sp.scratch((1, V7X_SUBLANES, d), F32),
        pltpu.SemaphoreType.DMA((2,)),
        pltpu.SemaphoreType.DMA((1,)),
    ]
    return pl.pallas_call(
        functools.partial(_ffn_kernel, tm=tm, nf=nf, per_stream=per_stream),
        grid=(m // tm, nf),
        in_specs=in_specs,
        out_specs=pl.BlockSpec(memory_space=pl.ANY),
        out_shape=jax.ShapeDtypeStruct((m, d), F32),
        scratch_shapes=scratch,
        compiler_params=sp.params(("arbitrary", "arbitrary"), temp_bytes=4 * _nbytes((tm, tf), F32)),
        name="ffn",
    )(h2, wg, wu, wd, x1, mod)


def kernel(x_prompt, x_sample, cache_k, cache_v, state_conv, c_prompt, c_sample, w_ada, b_ada, g_mix_pre,
           g_mix_post, w_in, rel_bias, w_dw, b_dw, g_conv_ln, b_conv_ln, g_attn_out, g_conv_out, w_out,
           g_ffn_pre, g_ffn_post, w_ffn_gate, w_ffn_up, w_ffn_down):
    batch, seq, d = x_prompt.shape
    dec_batch, dec_seq, _ = x_sample.shape
    depth, _, keep, n_heads, hd = cache_k.shape
    da = n_heads * hd
    dc = state_conv.shape[3]
    conv_state = state_conv.shape[2]
    assert batch == 1 and dec_seq == CHUNK and keep == N_PAST_CHUNKS * CHUNK and dec_batch == PROMPT_MOD_ROW
    assert seq % (Q_CHUNKS * CHUNK) == 0 and seq >= keep
    m_s = dec_batch * dec_seq

    vec = lambda a: a.reshape(depth, 1, a.shape[-1])
    g_mix_pre, g_mix_post, g_ffn_pre, g_ffn_post = map(vec, (g_mix_pre, g_mix_post, g_ffn_pre, g_ffn_post))
    b_dw, g_conv_ln, b_conv_ln, g_attn_out, g_conv_out = map(vec, (b_dw, g_conv_ln, b_conv_ln, g_attn_out, g_conv_out))

    c_all = jnp.concatenate(
        [c_sample, c_prompt, jnp.zeros((MOD_ROWS - dec_batch - batch, d), F32)], axis=0)
    mod = _ada(c_all, w_ada, b_ada, g_mix_pre, g_mix_post, g_ffn_pre, g_ffn_post)

    dff = w_ffn_gate.shape[2]
    band_bias, cache_bias, new_bias = _bias_tables(rel_bias)
    cache_k2 = cache_k.reshape(depth, dec_batch, keep * n_heads, hd)
    cache_v2 = cache_v.reshape(depth, dec_batch, keep * n_heads, hd)

    xp = x_prompt.reshape(seq, d)
    xs = x_sample.reshape(m_s, d)
    outs = {k: [] for k in ("conv_p", "k_p", "v_p", "conv_s", "k_s", "v_s")}
    tt_p = TT_CONV_PROMPT
    for l in range(depth):
        h = _prenorm(xp, mod, l, per_stream=False)
        qkv, wd_b = _proj(h, w_in, l, row0=0, m=seq, col0=0, n=3 * da, emit_bf16=True, emit_f32=False, tm=TM,
                          side=[_SideConvert(w_ffn_down, l, (dff // SIDE_ROW_BLOCKS, d // SIDE_COL_BLOCKS))])
        (kv_tail,) = _proj(h, w_in, l, row0=seq - keep, m=keep, col0=da, n=2 * da,
                           emit_bf16=False, emit_f32=True, tm=keep)
        (u,) = _glu(h, w_in, l, col_a=3 * da, col_b=3 * da + dc, n=dc, tm=TM)
        attn, wu_b, w_out_b = _attn_prompt(qkv, band_bias, l, n_heads, hd,
                                           side=[_SideConvert(w_ffn_up, l, (d // SIDE_BLOCKS, dff)),
                                                 _SideConvert(w_out, l, (w_out.shape[1] // SIDE_BLOCKS, d))])
        cat, wg_b = _conv_cat(u, None, attn, w_dw, b_dw, g_conv_ln, b_conv_ln, g_attn_out, g_conv_out, l, tt_p,
                              side=[_SideConvert(w_ffn_gate, l, (d // SIDE_BLOCKS, dff))])
        x1, h2 = _outproj(cat, w_out_b, xp, mod, l, False, TM_OUT)
        xp = _ffn(h2, wg_b, wu_b, wd_b, x1, mod, l, False, TM)
        outs["conv_p"].append(u[seq - conv_state:].reshape(batch, conv_state, dc))
        outs["k_p"].append(kv_tail[:, :da].reshape(batch, keep, n_heads, hd))
        outs["v_p"].append(kv_tail[:, da:].reshape(batch, keep, n_heads, hd))

        h = _prenorm(xs, mod, l, per_stream=True)
        (q_s,) = _proj(h, w_in, l, row0=0, m=m_s, col0=0, n=da, emit_bf16=True, emit_f32=False, tm=TM)
        kv_s, kv_s32 = _proj(h, w_in, l, row0=0, m=m_s, col0=da, n=2 * da, emit_bf16=True, emit_f32=True, tm=TM)
        (u,) = _glu(h, w_in, l, col_a=3 * da, col_b=3 * da + dc, n=dc, tm=TM)
        attn = _attn_sample(q_s, kv_s, cache_k2, cache_v2, cache_bias, new_bias, l, n_heads, hd)
        (cat,) = _conv_cat(u, state_conv, attn, w_dw, b_dw, g_conv_ln, b_conv_ln, g_attn_out, g_conv_out, l, CHUNK)
        x1, h2 = _outproj(cat, w_out_b, xs, mod, l, True, TM_OUT)
        xs = _ffn(h2, wg_b, wu_b, wd_b, x1, mod, l, True, TM)
        outs["conv_s"].append(u.reshape(dec_batch, dec_seq, dc)[:, dec_seq - conv_state:, :])
        outs["k_s"].append(kv_s32[:, :da].reshape(dec_batch, dec_seq, n_heads, hd))
        outs["v_s"].append(kv_s32[:, da:].reshape(dec_batch, dec_seq, n_heads, hd))

    stack = lambda name: jnp.stack(outs[name])
    return (xp.reshape(batch, seq, d), xs.reshape(dec_batch, dec_seq, d),
            stack("conv_p"), stack("k_p"), stack("v_p"), stack("conv_s"), stack("k_s"), stack("v_s"))
```

```python
import functools

import numpy as np
import jax
import jax.numpy as jnp
from jax import lax
from jax.experimental import pallas as pl
from jax.experimental.pallas import tpu as pltpu

F32 = jnp.float32
BF16 = jnp.bfloat16

EPS = 1e-6
NEG_INF = -1e30
LOG2E = 1.4426950408889634
CHUNK = 64
N_PAST_CHUNKS = 8
Q_CHUNKS = 4
BAND_CHUNKS = Q_CHUNKS + N_PAST_CHUNKS
N_MOD = 6
MOD_SH_M, MOD_SC_M, MOD_GT_M, MOD_SH_F, MOD_SC_F, MOD_GT_F = range(N_MOD)

V7X_VMEM_BYTES = 64 * 1024 * 1024
V7X_VMEM_REQUEST_CAP = 60 * 1024 * 1024
V7X_SPILL_ALLOWANCE = 4 * 1024 * 1024
V7X_LANES = 128
V7X_SUBLANES = 8
V7X_BF16_SUBLANES = 16

STRIP = 4 * V7X_BF16_SUBLANES
CONV_STRIP = 2 * V7X_BF16_SUBLANES
HALO = 32
NORM_UNROLL = 1

TM = 1024
TN_PROJ = 512
TN_GLU = 256
TM_OUT = 512
TK_OUT = 512
TF_FFN = 256
FFN_CHUNK = 128
TM_EW = 512
TN_ADA = 512
TT_CONV_PROMPT = 128
SIDE_BLOCKS = 64
ATTN_SIDE_BLOCKS = 128
SIDE_ROW_BLOCKS, SIDE_COL_BLOCKS = 16, 4
ATTN_HEADS_PER_STEP = 4
MOD_ROWS = 48
PROMPT_MOD_ROW = 32


def _sigmoid(x):
    return 1.0 / (1.0 + jnp.exp(-x))


def _nbytes(shape, dtype):
    return int(np.prod(shape)) * jnp.dtype(dtype).itemsize


class _Specs:
    def __init__(self):
        self.bytes = 0

    def block(self, shape, index_map, dtype, buffers=2):
        real = tuple(s for s in shape if s is not None)
        self.bytes += buffers * _nbytes(real, dtype)
        if buffers == 1:
            return pl.BlockSpec(shape, index_map, pipeline_mode=pl.Buffered(1))
        return pl.BlockSpec(shape, index_map)

    def scratch(self, shape, dtype):
        self.bytes += _nbytes(shape, dtype)
        return pltpu.VMEM(shape, dtype)

    def params(self, semantics, temp_bytes=0):
        limit = min(self.bytes + temp_bytes + V7X_SPILL_ALLOWANCE, V7X_VMEM_REQUEST_CAP)
        return pltpu.CompilerParams(dimension_semantics=semantics, vmem_limit_bytes=limit)


def _ada_kernel(c_ref, w_ref, b_ref, gmpre_ref, gmpost_ref, gfpre_ref, gfpost_ref, o_ref, *, blocks_per_mod):
    which = pl.program_id(1) // blocks_per_mod
    c = c_ref[...]
    a = (c * _sigmoid(c)).astype(BF16)
    w = w_ref[...].astype(BF16)
    m = jnp.dot(a, w, preferred_element_type=F32) + b_ref[...]
    g_scale = jnp.where(which == MOD_SC_M, gmpre_ref[...], gfpre_ref[...])
    g_gate = jnp.where(which == MOD_GT_M, gmpost_ref[...], gfpost_ref[...])
    is_scale = jnp.logical_or(which == MOD_SC_M, which == MOD_SC_F)
    is_gate = jnp.logical_or(which == MOD_GT_M, which == MOD_GT_F)
    o_ref[...] = jnp.where(is_scale, g_scale * (1.0 + m), jnp.where(is_gate, g_gate * m, m))


def _ada(c_all, w_ada, b_ada, g_mix_pre, g_mix_post, g_ffn_pre, g_ffn_post):
    depth, d, n = w_ada.shape
    rows = c_all.shape[0]
    blocks_per_mod = d // TN_ADA
    sp = _Specs()
    gain = lambda: sp.block((None, 1, TN_ADA), lambda l, j: (l, 0, j % blocks_per_mod), F32)
    in_specs = [
        sp.block((rows, d), lambda l, j: (0, 0), F32),
        sp.block((None, d, TN_ADA), lambda l, j: (l, 0, j), F32),
        sp.block((None, 1, TN_ADA), lambda l, j: (l, 0, j), F32),
        gain(), gain(), gain(), gain(),
    ]
    out_spec = sp.block((None, rows, TN_ADA), lambda l, j: (l, 0, j), F32)
    return pl.pallas_call(
        functools.partial(_ada_kernel, blocks_per_mod=blocks_per_mod),
        grid=(depth, n // TN_ADA),
        in_specs=in_specs,
        out_specs=out_spec,
        out_shape=jax.ShapeDtypeStruct((depth, rows, n), F32),
        compiler_params=sp.params(("parallel", "parallel"), temp_bytes=_nbytes((d, TN_ADA), BF16)),
        name="ada_modulation",
    )(c_all, w_ada, b_ada.reshape(depth, 1, n), g_mix_pre, g_mix_post, g_ffn_pre, g_ffn_post)


def _mod_block(sp, layer, which, tm, d, per_stream, grid_rank):
    if per_stream:
        rows = tm // CHUNK
        assert rows % 8 == 0
        if grid_rank == 1:
            imap = lambda i: (layer, i, which)
        else:
            imap = lambda i, j: (layer, i, which)
    else:
        rows = 8
        blk = PROMPT_MOD_ROW // 8
        if grid_rank == 1:
            imap = lambda i: (layer, blk, which)
        else:
            imap = lambda i, j: (layer, blk, which)
    return sp.block((None, rows, d), imap, F32)


def _strip_loops(tm, per_stream, vec_refs, bc_ref, strip_fn):
    rows_per_vec = CHUNK if per_stream else tm
    n_inner = rows_per_vec // STRIP

    def outer(s, carry):
        for k, ref in enumerate(vec_refs):
            bc_ref[k] = jnp.broadcast_to(ref[pl.ds(s, 1), :], bc_ref.shape[1:])

        def inner(j, c2):
            strip_fn(pl.ds(pl.multiple_of(s * rows_per_vec + j * STRIP, STRIP), STRIP))
            return c2
        lax.fori_loop(0, n_inner, inner, 0, unroll=min(n_inner, NORM_UNROLL))
        return carry
    lax.fori_loop(0, tm // rows_per_vec, outer, 0)


def _split(x):
    return x.reshape(x.shape[0] // V7X_SUBLANES, V7X_SUBLANES, x.shape[-1])


def _merge(x3):
    return x3.reshape(x3.shape[0] * V7X_SUBLANES, x3.shape[-1])


def _inv_rms(x):
    return lax.rsqrt(jnp.mean(x * x, axis=-1, keepdims=True) + EPS)


def _prenorm_kernel(x_ref, sc_ref, sh_ref, h_ref, bc_ref, *, tm, per_stream):
    def strip(r):
        x = _split(x_ref[r, :])
        h_ref[r, :] = _merge(x * _inv_rms(x) * bc_ref[0] + bc_ref[1]).astype(BF16)
    _strip_loops(tm, per_stream, [sc_ref, sh_ref], bc_ref, strip)


def _prenorm(x, mod, layer, per_stream):
    m, d = x.shape
    tm = TM_EW
    sp = _Specs()
    in_specs = [
        sp.block((tm, d), lambda i: (i, 0), F32),
        _mod_block(sp, layer, MOD_SC_M, tm, d, per_stream, 1),
        _mod_block(sp, layer, MOD_SH_M, tm, d, per_stream, 1),
    ]
    out_spec = sp.block((tm, d), lambda i: (i, 0), BF16)
    return pl.pallas_call(
        functools.partial(_prenorm_kernel, tm=tm, per_stream=per_stream),
        grid=(m // tm,),
        in_specs=in_specs,
        out_specs=out_spec,
        out_shape=jax.ShapeDtypeStruct((m, d), BF16),
        scratch_shapes=[sp.scratch((2, V7X_SUBLANES, d), F32)],
        compiler_params=sp.params(("parallel",)),
        name="prenorm",
    )(x, mod, mod)


class _SideConvert:
    def __init__(self, src, layer, block):
        self.src, self.layer, self.block = src, layer, block
        rows, cols = src.shape[1:]
        br, bc = block
        assert rows % br == 0 and cols % bc == 0 and br % V7X_BF16_SUBLANES == 0 and bc % V7X_LANES == 0
        self.per_row = cols // bc
        self.n_blocks = (rows // br) * self.per_row

    def specs(self, sp, grid):
        n_steps = int(np.prod(grid))
        assert n_steps >= self.n_blocks
        last, per_row, layer = self.n_blocks - 1, self.per_row, self.layer

        def block_of(*idx):
            step = idx[0]
            for size, k in zip(grid[1:], idx[1:]):
                step = step * size + k
            b = jnp.minimum(step, last)
            return b // per_row, b % per_row
        in_spec = sp.block((None,) + self.block, lambda *idx: (layer,) + block_of(*idx), self.src.dtype)
        out_spec = sp.block(self.block, block_of, BF16)
        return in_spec, out_spec, jax.ShapeDtypeStruct(self.src.shape[1:], BF16)


def _side_specs(sp, side, grid):
    specs = [c.specs(sp, grid) for c in side]
    return [s[0] for s in specs], [s[1] for s in specs], [s[2] for s in specs], [c.src for c in side]


def _run_side_converts(refs, n_side):
    for src_ref, dst_ref in zip(refs[:n_side], refs[len(refs) - n_side:]):
        dst_ref[...] = src_ref[...].astype(BF16)


def _proj_kernel(h_ref, w_ref, *refs, emit_bf16, emit_f32, n_side):
    acc = jnp.dot(h_ref[...], w_ref[...].astype(BF16), preferred_element_type=F32)
    outs = list(refs[n_side:])
    if emit_bf16:
        outs.pop(0)[...] = acc.astype(BF16)
    if emit_f32:
        outs.pop(0)[...] = acc
    _run_side_converts(refs, n_side)


def _proj(h, w, layer, *, row0, m, col0, n, emit_bf16, emit_f32, tm, side=()):
    d = h.shape[1]
    tn = TN_PROJ
    assert row0 % tm == 0 and m % tm == 0 and col0 % tn == 0 and n % tn == 0
    rb, cb = row0 // tm, col0 // tn
    grid = (m // tm, n // tn)
    sp = _Specs()
    side_in, side_out, side_shape, side_args = _side_specs(sp, side, grid)
    in_specs = [
        sp.block((tm, d), lambda i, j: (i + rb, 0), BF16),
        sp.block((None, d, tn), lambda i, j: (layer, 0, j + cb), w.dtype),
    ] + side_in
    out_specs, out_shape = [], []
    if emit_bf16:
        out_specs.append(sp.block((tm, tn), lambda i, j: (i, j), BF16))
        out_shape.append(jax.ShapeDtypeStruct((m, n), BF16))
    if emit_f32:
        out_specs.append(sp.block((tm, tn), lambda i, j: (i, j), F32))
        out_shape.append(jax.ShapeDtypeStruct((m, n), F32))
    return pl.pallas_call(
        functools.partial(_proj_kernel, emit_bf16=emit_bf16, emit_f32=emit_f32, n_side=len(side)),
        grid=grid,
        in_specs=in_specs,
        out_specs=out_specs + side_out,
        out_shape=out_shape + side_shape,
        compiler_params=sp.params(("arbitrary", "arbitrary"),
                                  temp_bytes=_nbytes((tm, tn), F32) + _nbytes((d, tn), BF16)),
        name="in_proj",
    )(h, w, *side_args)


def _glu_kernel(h_ref, wa_ref, wb_ref, *refs, n_side):
    h = h_ref[...]
    a = jnp.dot(h, wa_ref[...].astype(BF16), preferred_element_type=F32)
    b = jnp.dot(h, wb_ref[...].astype(BF16), preferred_element_type=F32)
    refs[n_side][...] = a * _sigmoid(b)
    _run_side_converts(refs, n_side)


def _glu(h, w, layer, *, col_a, col_b, n, tm, side=()):
    m, d = h.shape
    tn = TN_GLU
    ca, cb = col_a // tn, col_b // tn
    grid = (m // tm, n // tn)
    sp = _Specs()
    side_in, side_out, side_shape, side_args = _side_specs(sp, side, grid)
    in_specs = [
        sp.block((tm, d), lambda i, j: (i, 0), BF16),
        sp.block((None, d, tn), lambda i, j: (layer, 0, j + ca), w.dtype),
        sp.block((None, d, tn), lambda i, j: (layer, 0, j + cb), w.dtype),
    ] + side_in
    out_spec = sp.block((tm, tn), lambda i, j: (i, j), F32)
    return pl.pallas_call(
        functools.partial(_glu_kernel, n_side=len(side)),
        grid=grid,
        in_specs=in_specs,
        out_specs=[out_spec] + side_out,
        out_shape=[jax.ShapeDtypeStruct((m, n), F32)] + side_shape,
        compiler_params=sp.params(("arbitrary", "arbitrary"),
                                  temp_bytes=3 * _nbytes((tm, tn), F32) + 2 * _nbytes((d, tn), BF16)),
        name="in_proj_glu",
    )(h, w, w, *side_args)


def _softmax_pv(parts):
    m = None
    for s, _ in parts:
        mi = jnp.max(s, axis=-1, keepdims=True)
        m = mi if m is None else jnp.maximum(m, mi)
    l, o = None, None
    for s, v in parts:
        p = jnp.exp2(s - m)
        li = jnp.sum(p, axis=-1, keepdims=True)
        oi = jnp.dot(p.astype(BF16), v, preferred_element_type=F32)
        l = li if l is None else l + li
        o = oi if o is None else o + oi
    return o / l


def _qk(q, k):
    return lax.dot_general(q, k, (((1,), (1,)), ((), ())), preferred_element_type=F32)


def _attn_prompt_kernel(q_ref, k_ref, v_ref, strip_ref, *refs, scale, hd, heads, n_side):
    g = pl.program_id(1)
    rows = Q_CHUNKS * CHUNK
    window = BAND_CHUNKS * CHUNK
    o_ref, b_ref = refs[n_side], refs[-1]
    _run_side_converts(refs[:-1], n_side)

    @pl.when(g == 0)
    def _():
        for h in range(heads):
            for qi in range(Q_CHUNKS):
                off = (Q_CHUNKS - 1 - qi) * CHUNK
                b_ref[h, qi * CHUNK:(qi + 1) * CHUNK, :] = strip_ref[h, :, off:off + window]

    def run(k0, n_keys, b_off):
        for h in range(heads):
            c = slice(h * hd, (h + 1) * hd)
            k = k_ref[pl.ds(k0, n_keys), c]
            v = v_ref[pl.ds(k0, n_keys), c]
            s = _qk(q_ref[:, c], k) * scale + b_ref[h, :, b_off:b_off + n_keys]
            o_ref[:, c] = _softmax_pv([(s, v)])

    n_lead = N_PAST_CHUNKS // Q_CHUNKS
    for lead in range(n_lead):
        keys = (lead + 1) * rows
        pl.when(g == lead)(functools.partial(run, 0, keys, window - keys))

    @pl.when(g >= n_lead)
    def _():
        run(pl.multiple_of((g - n_lead) * rows, rows), window, 0)


def _attn_prompt(qkv, bias, layer, n_heads, hd, side=()):
    t = qkv.shape[0]
    rows = Q_CHUNKS * CHUNK
    keys = BAND_CHUNKS * CHUNK
    heads = ATTN_HEADS_PER_STEP
    assert n_heads % heads == 0
    n_groups = n_heads // heads
    width = heads * hd
    grid = (n_groups, t // rows)
    sp = _Specs()
    side_in, side_out, side_shape, side_args = _side_specs(sp, side, grid)
    in_specs = [
        sp.block((rows, width), lambda h, g: (g, h), BF16),
        sp.block((t, width), lambda h, g: (0, n_groups + h), BF16),
        sp.block((t, width), lambda h, g: (0, 2 * n_groups + h), BF16),
        sp.block((None, heads, CHUNK, bias.shape[-1]), lambda h, g: (layer, h, 0, 0), F32),
    ] + side_in
    out_spec = sp.block((rows, width), lambda h, g: (g, h), F32)
    return pl.pallas_call(
        functools.partial(_attn_prompt_kernel, scale=hd ** -0.5 * LOG2E, hd=hd, heads=heads, n_side=len(side)),
        grid=grid,
        in_specs=in_specs,
        out_specs=[out_spec] + side_out,
        out_shape=[jax.ShapeDtypeStruct((t, n_heads * hd), F32)] + side_shape,
        scratch_shapes=[sp.scratch((heads, rows, keys), F32)],
        compiler_params=sp.params(("arbitrary", "arbitrary"), temp_bytes=6 * _nbytes((rows, keys), F32)),
        name="attn_prompt",
    )(qkv, qkv, qkv, bias, *side_args)


def _attn_sample_kernel(q_ref, kn_ref, vn_ref, kc_ref, vc_ref, bc_ref, bn_ref, o_ref, *, scale, n_heads, hd):
    keep = kc_ref.shape[0] // n_heads
    for h in range(n_heads):
        c = slice(h * hd, (h + 1) * hd)
        frames = pl.ds(h, keep, stride=n_heads)
        q = q_ref[:, c]
        s_cache = _qk(q, kc_ref[frames, :].astype(BF16)) * scale + bc_ref[h]
        s_new = _qk(q, kn_ref[:, c]) * scale + bn_ref[h]
        o_ref[:, c] = _softmax_pv([(s_cache, vc_ref[frames, :].astype(BF16)), (s_new, vn_ref[:, c])])


def _attn_sample(q, kv, cache_k, cache_v, bias_cache, bias_new, layer, n_heads, hd):
    m, da = q.shape
    n_streams = m // CHUNK
    keep = cache_k.shape[2] // n_heads
    sp = _Specs()
    in_specs = [
        sp.block((CHUNK, da), lambda b: (b, 0), BF16),
        sp.block((CHUNK, da), lambda b: (b, 0), BF16),
        sp.block((CHUNK, da), lambda b: (b, 1), BF16),
        sp.block((None, None, keep * n_heads, hd), lambda b: (layer, b, 0, 0), F32),
        sp.block((None, None, keep * n_heads, hd), lambda b: (layer, b, 0, 0), F32),
        sp.block((None, n_heads, CHUNK, keep), lambda b: (layer, 0, 0, 0), F32),
        sp.block((None, n_heads, CHUNK, CHUNK), lambda b: (layer, 0, 0, 0), F32),
    ]
    out_spec = sp.block((CHUNK, da), lambda b: (b, 0), F32)
    return pl.pallas_call(
        functools.partial(_attn_sample_kernel, scale=hd ** -0.5 * LOG2E, n_heads=n_heads, hd=hd),
        grid=(n_streams,),
        in_specs=in_specs,
        out_specs=out_spec,
        out_shape=jax.ShapeDtypeStruct((m, da), F32),
        compiler_params=sp.params(("parallel",), temp_bytes=4 * 1024 * 1024),
        name="attn_sample",
    )(q, kv, kv, cache_k, cache_v, bias_cache, bias_new)


def _bias_tables(rel_bias):
    depth, n_heads, n_rel = rel_bias.shape
    max_rel = (n_rel - 1) // 2
    keep = N_PAST_CHUNKS * CHUNK
    lead = (Q_CHUNKS - 1) * CHUNK
    cols = BAND_CHUNKS * CHUNK
    width = lead + cols
    p = CHUNK + width - 1
    x = np.arange(p)
    rel = np.where(x < width, -x, p - x) + keep + lead
    diag = rel_bias[:, :, np.clip(rel, -max_rel, max_rel) + max_rel]
    a = jnp.tile(diag, (1, 1, CHUNK))[:, :, :CHUNK * (p - 1)].reshape(depth, n_heads, CHUNK, p - 1)
    strip = a[..., :width] * LOG2E
    col = np.arange(width)
    masked = jnp.where((col >= lead) & (col < lead + keep + CHUNK), strip, NEG_INF)
    lane_pad = -width % V7X_LANES
    masked = jnp.pad(masked, ((0, 0), (0, 0), (0, 0), (0, lane_pad)), constant_values=NEG_INF)
    return masked, strip[..., lead:lead + keep], strip[..., lead + keep:lead + keep + CHUNK]


def _conv_cat_kernel(u_ref, hist_ref, attn_ref, wdw_ref, bdw_ref, gln_ref, bln_ref, gao_ref, gco_ref,
                     *refs, tt, conv_width, hist_is_state, n_side):
    xs_ref, y_ref, bc_ref = refs[-3:]
    cat_ref = refs[n_side]
    _run_side_converts(refs[:-3], n_side)
    da = attn_ref.shape[1]
    dc = u_ref.shape[1]
    if hist_is_state:
        n_hist = hist_ref.shape[0]
        xs_ref[0, 0:V7X_SUBLANES, :] = jnp.zeros((V7X_SUBLANES, dc), F32)
        xs_ref[0, HALO - n_hist:HALO, :] = hist_ref[...]
    else:
        xs_ref[0, 0:HALO, :] = jnp.where(pl.program_id(0) > 0, hist_ref[...], 0.0)
    xs_ref[0, HALO:HALO + tt, :] = u_ref[...]
    first = HALO - (conv_width - 1)
    n_shift = HALO + tt - V7X_SUBLANES
    for r in range(1, V7X_SUBLANES):
        xs_ref[r, 0:n_shift, :] = xs_ref[0, r:r + n_shift, :]
    rc_rows = 64
    for rc in range(tt // rc_rows):
        for cb in range(dc // V7X_LANES):
            c = slice(cb * V7X_LANES, (cb + 1) * V7X_LANES)
            acc = None
            for w in range(conv_width):
                r0 = rc * rc_rows + (first + w) // V7X_SUBLANES * V7X_SUBLANES
                term = xs_ref[(first + w) % V7X_SUBLANES, r0:r0 + rc_rows, c] * wdw_ref[w:w + 1, c]
                acc = term if acc is None else acc + term
            y_ref[rc * rc_rows:(rc + 1) * rc_rows, c] = acc + bdw_ref[:, c]

    for k, ref in enumerate((gln_ref, bln_ref, gco_ref, gao_ref)):
        bc_ref[k] = jnp.broadcast_to(ref[...], bc_ref.shape[1:])

    def strip(j, carry):
        r = pl.ds(pl.multiple_of(j * CONV_STRIP, CONV_STRIP), CONV_STRIP)
        y = _split(y_ref[r, :])
        yc = y - jnp.mean(y, axis=-1, keepdims=True)
        yn = yc * _inv_rms(yc) * bc_ref[0] + bc_ref[1]
        z = yn * _sigmoid(yn)
        cat_ref[r, da:da + dc] = _merge(z * _inv_rms(z) * bc_ref[2]).astype(BF16)
        a = _split(attn_ref[r, :])
        cat_ref[r, 0:da] = _merge(a * _inv_rms(a) * bc_ref[3]).astype(BF16)
        return carry
    n_strips = tt // CONV_STRIP
    lax.fori_loop(0, n_strips, strip, 0, unroll=2 if n_strips > 2 else 1)


def _conv_cat(u, state, attn, w_dw, b_dw, g_ln, b_ln, g_ao, g_co, layer, tt, side=()):
    m, dc = u.shape
    da = attn.shape[1]
    conv_width = w_dw.shape[1]
    assert conv_width - 1 <= HALO and tt % HALO == 0
    assert da == dc
    grid = (m // tt,)
    sp = _Specs()
    side_in, side_out, side_shape, side_args = _side_specs(sp, side, grid)
    vec = lambda width: sp.block((None, 1, width), lambda i: (layer, 0, 0), F32)
    if state is None:
        hist, hist_spec = u, sp.block((HALO, dc), lambda i: (jnp.maximum(i * (tt // HALO) - 1, 0), 0), F32)
    else:
        assert state.shape[1:] == (m // tt, conv_width - 1, dc)
        hist, hist_spec = state, sp.block((None, None, conv_width - 1, dc), lambda i: (layer, i, 0, 0), F32)
    in_specs = [
        sp.block((tt, dc), lambda i: (i, 0), F32),
        hist_spec,
        sp.block((tt, da), lambda i: (i, 0), F32),
        sp.block((None, conv_width, dc), lambda i: (layer, 0, 0), F32),
        vec(dc), vec(dc), vec(dc), vec(da), vec(dc),
    ] + side_in
    out_spec = sp.block((tt, da + dc), lambda i: (i, 0), BF16)
    scratch = [sp.scratch((V7X_SUBLANES, HALO + tt, dc), F32), sp.scratch((tt, dc), F32),
               sp.scratch((4, V7X_SUBLANES, dc), F32)]
    return pl.pallas_call(
        functools.partial(_conv_cat_kernel, tt=tt, conv_width=conv_width, hist_is_state=state is not None,
                          n_side=len(side)),
        grid=grid,
        in_specs=in_specs,
        out_specs=[out_spec] + side_out,
        out_shape=[jax.ShapeDtypeStruct((m, da + dc), BF16)] + side_shape,
        scratch_shapes=scratch,
        compiler_params=sp.params(("arbitrary",), temp_bytes=4 * _nbytes((tt, dc), F32)),
        name="conv_cat",
    )(u, hist, attn, w_dw, b_dw, g_ln, b_ln, g_ao, g_co, *side_args)


def _outproj_kernel(cat_ref, w_ref, x_ref, gt_ref, sc_ref, sh_ref, x1_ref, h2_ref, bc_ref, *, tm, nk, per_stream):
    k = pl.program_id(1)

    @pl.when(k == 0)
    def _():
        x1_ref[...] = jnp.dot(cat_ref[...], w_ref[...], preferred_element_type=F32)

    @pl.when(k > 0)
    def _():
        x1_ref[...] += jnp.dot(cat_ref[...], w_ref[...], preferred_element_type=F32)

    @pl.when(k == nk - 1)
    def _():
        def strip(r):
            mixed = _split(x1_ref[r, :])
            x1 = _split(x_ref[r, :]) + mixed * _inv_rms(mixed) * bc_ref[0]
            x1_ref[r, :] = _merge(x1)
            h2_ref[r, :] = _merge(x1 * _inv_rms(x1) * bc_ref[1] + bc_ref[2]).astype(BF16)
        _strip_loops(tm, per_stream, [gt_ref, sc_ref, sh_ref], bc_ref, strip)


def _outproj(cat, w_out, x, mod, layer, per_stream, tm):
    m, dmix = cat.shape
    d = w_out.shape[1]
    tk = TK_OUT
    nk = dmix // tk
    sp = _Specs()
    in_specs = [
        sp.block((tm, tk), lambda i, k: (i, k), BF16),
        sp.block((tk, d), lambda i, k: (k, 0), BF16),
        sp.block((tm, d), lambda i, k: (i, 0), F32),
        _mod_block(sp, layer, MOD_GT_M, tm, d, per_stream, 2),
        _mod_block(sp, layer, MOD_SC_F, tm, d, per_stream, 2),
        _mod_block(sp, layer, MOD_SH_F, tm, d, per_stream, 2),
    ]
    out_specs = [
        sp.block((tm, d), lambda i, k: (i, 0), F32),
        sp.block((tm, d), lambda i, k: (i, 0), BF16),
    ]
    return pl.pallas_call(
        functools.partial(_outproj_kernel, tm=tm, nk=nk, per_stream=per_stream),
        grid=(m // tm, nk),
        in_specs=in_specs,
        out_specs=out_specs,
        out_shape=[jax.ShapeDtypeStruct((m, d), F32), jax.ShapeDtypeStruct((m, d), BF16)],
        scratch_shapes=[sp.scratch((3, V7X_SUBLANES, d), F32)],
        compiler_params=sp.params(("parallel", "arbitrary")),
        name="out_proj",
    )(cat, w_out, x, mod, mod, mod)


def _ffn_kernel(h_ref, wg_ref, wu_ref, wd_ref, x1_hbm, gt_ref, x2_hbm, acc_ref, xbuf_ref, bc_ref, in_sem, out_sem,
                *, tm, nf, per_stream):
    i = pl.program_id(0)
    f = pl.program_id(1)
    n_chunks = tm // FFN_CHUNK
    strips = FFN_CHUNK // STRIP

    def chunk_rows(c):
        return pl.ds(pl.multiple_of(i * tm + c * FFN_CHUNK, FFN_CHUNK), FFN_CHUNK)

    def x1_copy(c, slot):
        return pltpu.make_async_copy(x1_hbm.at[chunk_rows(c), :], xbuf_ref.at[slot], in_sem.at[slot])

    def out_copy(c):
        src = acc_ref.at[pl.ds(pl.multiple_of(c * FFN_CHUNK, FFN_CHUNK), FFN_CHUNK), :]
        return pltpu.make_async_copy(src, x2_hbm.at[chunk_rows(c), :], out_sem.at[0])

    @pl.when(f == 0)
    def _():
        acc_ref[...] = jnp.zeros_like(acc_ref)

    @pl.when(f == nf - 1)
    def _():
        x1_copy(0, 0).start()
        x1_copy(1, 1).start()

    h = h_ref[...]
    g = jnp.dot(h, wg_ref[...], preferred_element_type=F32)
    u = jnp.dot(h, wu_ref[...], preferred_element_type=F32)
    a = (g * _sigmoid(g) * u).astype(BF16)
    acc_ref[...] += jnp.dot(a, wd_ref[...], preferred_element_type=F32)

    @pl.when(f == nf - 1)
    def _():
        def gate(row):
            bc_ref[0] = jnp.broadcast_to(gt_ref[pl.ds(row, 1), :], bc_ref.shape[1:])
        if not per_stream:
            gate(0)

        def chunk(c, carry):
            slot = c % 2
            x1_copy(c, slot).wait()
            for t in range(strips):
                if per_stream:
                    gate(c * strips + t)
                r = pl.ds(pl.multiple_of(c * FFN_CHUNK + t * STRIP, STRIP), STRIP)
                y = _split(acc_ref[r, :])
                x1 = _split(xbuf_ref[slot, t * STRIP:(t + 1) * STRIP, :])
                acc_ref[r, :] = _merge(x1 + y * _inv_rms(y) * bc_ref[0])
            out_copy(c).start()

            @pl.when(c + 2 < n_chunks)
            def _():
                x1_copy(c + 2, slot).start()
            return carry
        lax.fori_loop(0, n_chunks, chunk, 0)
        for c in range(n_chunks):
            out_copy(c).wait()


def _ffn(h2, wg, wu, wd, x1, mod, layer, per_stream, tm):
    m, d = h2.shape
    dff = wg.shape[1]
    tf = TF_FFN
    nf = dff // tf
    assert dff % tf == 0 and tm % (2 * FFN_CHUNK) == 0 and FFN_CHUNK % STRIP == 0
    assert STRIP == CHUNK or not per_stream
    sp = _Specs()
    in_specs = [
        sp.block((tm, d), lambda i, f: (i, 0), BF16),
        sp.block((d, tf), lambda i, f: (0, f), BF16),
        sp.block((d, tf), lambda i, f: (0, f), BF16),
        sp.block((tf, d), lambda i, f: (f, 0), BF16),
        pl.BlockSpec(memory_space=pl.ANY),
        _mod_block(sp, layer, MOD_GT_F, tm, d, per_stream, 2),
    ]
    scratch = [
        sp.scratch((tm, d), F32),
        sp.scratch((2, FFN_CHUNK, d), F32),
        sp.scratch((1, V7X_SUBLANES, d), F32),
        pltpu.SemaphoreType.DMA((2,)),
        pltpu.SemaphoreType.DMA((1,)),
    ]
    return pl.pallas_call(
        functools.partial(_ffn_kernel, tm=tm, nf=nf, per_stream=per_stream),
        grid=(m // tm, nf),
        in_specs=in_specs,
        out_specs=pl.BlockSpec(memory_space=pl.ANY),
        out_shape=jax.ShapeDtypeStruct((m, d), F32),
        scratch_shapes=scratch,
        compiler_params=sp.params(("arbitrary", "arbitrary"), temp_bytes=4 * _nbytes((tm, tf), F32)),
        name="ffn",
    )(h2, wg, wu, wd, x1, mod)


def kernel(x_prompt, x_sample, cache_k, cache_v, state_conv, c_prompt, c_sample, w_ada, b_ada, g_mix_pre,
           g_mix_post, w_in, rel_bias, w_dw, b_dw, g_conv_ln, b_conv_ln, g_attn_out, g_conv_out, w_out,
           g_ffn_pre, g_ffn_post, w_ffn_gate, w_ffn_up, w_ffn_down):
    batch, seq, d = x_prompt.shape
    dec_batch, dec_seq, _ = x_sample.shape
    depth, _, keep, n_heads, hd = cache_k.shape
    da = n_heads * hd
    dc = state_conv.shape[3]
    conv_state = state_conv.shape[2]
    assert batch == 1 and dec_seq == CHUNK and keep == N_PAST_CHUNKS * CHUNK and dec_batch == PROMPT_MOD_ROW
    assert seq % (Q_CHUNKS * CHUNK) == 0 and seq >= keep
    m_s = dec_batch * dec_seq

    vec = lambda a: a.reshape(depth, 1, a.shape[-1])
    g_mix_pre, g_mix_post, g_ffn_pre, g_ffn_post = map(vec, (g_mix_pre, g_mix_post, g_ffn_pre, g_ffn_post))
    b_dw, g_conv_ln, b_conv_ln, g_attn_out, g_conv_out = map(vec, (b_dw, g_conv_ln, b_conv_ln, g_attn_out, g_conv_out))

    c_all = jnp.concatenate(
        [c_sample, c_prompt, jnp.zeros((MOD_ROWS - dec_batch - batch, d), F32)], axis=0)
    mod = _ada(c_all, w_ada, b_ada, g_mix_pre, g_mix_post, g_ffn_pre, g_ffn_post)

    dff = w_ffn_gate.shape[2]
    band_bias, cache_bias, new_bias = _bias_tables(rel_bias)
    cache_k2 = cache_k.reshape(depth, dec_batch, keep * n_heads, hd)
    cache_v2 = cache_v.reshape(depth, dec_batch, keep * n_heads, hd)

    xp = x_prompt.reshape(seq, d)
    xs = x_sample.reshape(m_s, d)
    outs = {k: [] for k in ("conv_p", "k_p", "v_p", "conv_s", "k_s", "v_s")}
    tt_p = TT_CONV_PROMPT
    for l in range(depth):
        h = _prenorm(xp, mod, l, per_stream=False)
        qkv, wd_b = _proj(h, w_in, l, row0=0, m=seq, col0=0, n=3 * da, emit_bf16=True, emit_f32=False, tm=TM,
                          side=[_SideConvert(w_ffn_down, l, (dff // SIDE_ROW_BLOCKS, d // SIDE_COL_BLOCKS))])
        (kv_tail,) = _proj(h, w_in, l, row0=seq - keep, m=keep, col0=da, n=2 * da,
                           emit_bf16=False, emit_f32=True, tm=keep)
        (u,) = _glu(h, w_in, l, col_a=3 * da, col_b=3 * da + dc, n=dc, tm=TM)
        attn, wu_b, w_out_b = _attn_prompt(qkv, band_bias, l, n_heads, hd,
                                           side=[_SideConvert(w_ffn_up, l, (d // ATTN_SIDE_BLOCKS, dff)),
                                                 _SideConvert(w_out, l, (w_out.shape[1] // ATTN_SIDE_BLOCKS, d))])
        cat, wg_b = _conv_cat(u, None, attn, w_dw, b_dw, g_conv_ln, b_conv_ln, g_attn_out, g_conv_out, l, tt_p,
                              side=[_SideConvert(w_ffn_gate, l, (d // SIDE_BLOCKS, dff))])
        x1, h2 = _outproj(cat, w_out_b, xp, mod, l, False, TM_OUT)
        xp = _ffn(h2, wg_b, wu_b, wd_b, x1, mod, l, False, TM)
        outs["conv_p"].append(u[seq - conv_state:].reshape(batch, conv_state, dc))
        outs["k_p"].append(kv_tail[:, :da].reshape(batch, keep, n_heads, hd))
        outs["v_p"].append(kv_tail[:, da:].reshape(batch, keep, n_heads, hd))

        h = _prenorm(xs, mod, l, per_stream=True)
        (q_s,) = _proj(h, w_in, l, row0=0, m=m_s, col0=0, n=da, emit_bf16=True, emit_f32=False, tm=TM)
        kv_s, kv_s32 = _proj(h, w_in, l, row0=0, m=m_s, col0=da, n=2 * da, emit_bf16=True, emit_f32=True, tm=TM)
        (u,) = _glu(h, w_in, l, col_a=3 * da, col_b=3 * da + dc, n=dc, tm=TM)
        attn = _attn_sample(q_s, kv_s, cache_k2, cache_v2, cache_bias, new_bias, l, n_heads, hd)
        (cat,) = _conv_cat(u, state_conv, attn, w_dw, b_dw, g_conv_ln, b_conv_ln, g_attn_out, g_conv_out, l, CHUNK)
        x1, h2 = _outproj(cat, w_out_b, xs, mod, l, True, TM_OUT)
        xs = _ffn(h2, wg_b, wu_b, wd_b, x1, mod, l, True, TM)
        outs["conv_s"].append(u.reshape(dec_batch, dec_seq, dc)[:, dec_seq - conv_state:, :])
        outs["k_s"].append(kv_s32[:, :da].reshape(dec_batch, dec_seq, n_heads, hd))
        outs["v_s"].append(kv_s32[:, da:].reshape(dec_batch, dec_seq, n_heads, hd))

    stack = lambda name: jnp.stack(outs[name])
    return (xp.reshape(batch, seq, d), xs.reshape(dec_batch, dec_seq, d),
            stack("conv_p"), stack("k_p"), stack("v_p"), stack("conv_s"), stack("k_s"), stack("v_s"))
```

```python
import functools

import numpy as np
import jax
import jax.numpy as jnp
from jax import lax
from jax.experimental import pallas as pl
from jax.experimental.pallas import tpu as pltpu

F32 = jnp.float32
BF16 = jnp.bfloat16

EPS = 1e-6
NEG_INF = -1e30
LOG2E = 1.4426950408889634
CHUNK = 64
N_PAST_CHUNKS = 8
Q_CHUNKS = 4
BAND_CHUNKS = Q_CHUNKS + N_PAST_CHUNKS
N_MOD = 6
MOD_SH_M, MOD_SC_M, MOD_GT_M, MOD_SH_F, MOD_SC_F, MOD_GT_F = range(N_MOD)

V7X_VMEM_BYTES = 64 * 1024 * 1024
V7X_VMEM_REQUEST_CAP = 60 * 1024 * 1024
V7X_SPILL_ALLOWANCE = 4 * 1024 * 1024
V7X_LANES = 128
V7X_SUBLANES = 8
V7X_BF16_SUBLANES = 16

STRIP = 4 * V7X_BF16_SUBLANES
CONV_STRIP = 2 * V7X_BF16_SUBLANES
HALO = 32
NORM_UNROLL = 1

TM = 1024
TN_PROJ = 512
TN_GLU = 256
TM_OUT = 512
TK_OUT = 1024
TF_FFN = 256
FFN_CHUNK = 128
TM_EW = 512
TN_ADA = 512
TT_CONV_PROMPT = 128
SIDE_BLOCKS = 64
ATTN_SIDE_BLOCKS = 128
SIDE_ROW_BLOCKS, SIDE_COL_BLOCKS = 86, 1
ATTN_HEADS_PER_STEP = 4
MOD_ROWS = 48
PROMPT_MOD_ROW = 32


def _sigmoid(x):
    return 1.0 / (1.0 + jnp.exp(-x))


def _nbytes(shape, dtype):
    return int(np.prod(shape)) * jnp.dtype(dtype).itemsize


class _Specs:
    def __init__(self):
        self.bytes = 0

    def block(self, shape, index_map, dtype, buffers=2):
        real = tuple(s for s in shape if s is not None)
        self.bytes += buffers * _nbytes(real, dtype)
        if buffers == 1:
            return pl.BlockSpec(shape, index_map, pipeline_mode=pl.Buffered(1))
        return pl.BlockSpec(shape, index_map)

    def scratch(self, shape, dtype):
        self.bytes += _nbytes(shape, dtype)
        return pltpu.VMEM(shape, dtype)

    def params(self, semantics, temp_bytes=0):
        limit = min(self.bytes + temp_bytes + V7X_SPILL_ALLOWANCE, V7X_VMEM_REQUEST_CAP)
        return pltpu.CompilerParams(dimension_semantics=semantics, vmem_limit_bytes=limit)


def _ada_kernel(c_ref, w_ref, b_ref, gmpre_ref, gmpost_ref, gfpre_ref, gfpost_ref, o_ref, *, blocks_per_mod):
    which = pl.program_id(1) // blocks_per_mod
    c = c_ref[...]
    a = (c * _sigmoid(c)).astype(BF16)
    w = w_ref[...].astype(BF16)
    m = jnp.dot(a, w, preferred_element_type=F32) + b_ref[...]
    g_scale = jnp.where(which == MOD_SC_M, gmpre_ref[...], gfpre_ref[...])
    g_gate = jnp.where(which == MOD_GT_M, gmpost_ref[...], gfpost_ref[...])
    is_scale = jnp.logical_or(which == MOD_SC_M, which == MOD_SC_F)
    is_gate = jnp.logical_or(which == MOD_GT_M, which == MOD_GT_F)
    o_ref[...] = jnp.where(is_scale, g_scale * (1.0 + m), jnp.where(is_gate, g_gate * m, m))


def _ada(c_all, w_ada, b_ada, g_mix_pre, g_mix_post, g_ffn_pre, g_ffn_post):
    depth, d, n = w_ada.shape
    rows = c_all.shape[0]
    blocks_per_mod = d // TN_ADA
    sp = _Specs()
    gain = lambda: sp.block((None, 1, TN_ADA), lambda l, j: (l, 0, j % blocks_per_mod), F32)
    in_specs = [
        sp.block((rows, d), lambda l, j: (0, 0), F32),
        sp.block((None, d, TN_ADA), lambda l, j: (l, 0, j), F32),
        sp.block((None, 1, TN_ADA), lambda l, j: (l, 0, j), F32),
        gain(), gain(), gain(), gain(),
    ]
    out_spec = sp.block((None, rows, TN_ADA), lambda l, j: (l, 0, j), F32)
    return pl.pallas_call(
        functools.partial(_ada_kernel, blocks_per_mod=blocks_per_mod),
        grid=(depth, n // TN_ADA),
        in_specs=in_specs,
        out_specs=out_spec,
        out_shape=jax.ShapeDtypeStruct((depth, rows, n), F32),
        compiler_params=sp.params(("parallel", "parallel"), temp_bytes=_nbytes((d, TN_ADA), BF16)),
        name="ada_modulation",
    )(c_all, w_ada, b_ada.reshape(depth, 1, n), g_mix_pre, g_mix_post, g_ffn_pre, g_ffn_post)


def _mod_block(sp, layer, which, tm, d, per_stream, grid_rank):
    if per_stream:
        rows = tm // CHUNK
        assert rows % 8 == 0
        if grid_rank == 1:
            imap = lambda i: (layer, i, which)
        else:
            imap = lambda i, j: (layer, i, which)
    else:
        rows = 8
        blk = PROMPT_MOD_ROW // 8
        if grid_rank == 1:
            imap = lambda i: (layer, blk, which)
        else:
            imap = lambda i, j: (layer, blk, which)
    return sp.block((None, rows, d), imap, F32)


def _strip_loops(tm, per_stream, vec_refs, bc_ref, strip_fn):
    rows_per_vec = CHUNK if per_stream else tm
    n_inner = rows_per_vec // STRIP

    def outer(s, carry):
        for k, ref in enumerate(vec_refs):
            bc_ref[k] = jnp.broadcast_to(ref[pl.ds(s, 1), :], bc_ref.shape[1:])

        def inner(j, c2):
            strip_fn(pl.ds(pl.multiple_of(s * rows_per_vec + j * STRIP, STRIP), STRIP))
            return c2
        lax.fori_loop(0, n_inner, inner, 0, unroll=min(n_inner, NORM_UNROLL))
        return carry
    lax.fori_loop(0, tm // rows_per_vec, outer, 0)


def _split(x):
    return x.reshape(x.shape[0] // V7X_SUBLANES, V7X_SUBLANES, x.shape[-1])


def _merge(x3):
    return x3.reshape(x3.shape[0] * V7X_SUBLANES, x3.shape[-1])


def _inv_rms(x):
    return lax.rsqrt(jnp.mean(x * x, axis=-1, keepdims=True) + EPS)


def _prenorm_kernel(x_ref, sc_ref, sh_ref, h_ref, bc_ref, *, tm, per_stream):
    def strip(r):
        x = _split(x_ref[r, :])
        h_ref[r, :] = _merge(x * _inv_rms(x) * bc_ref[0] + bc_ref[1]).astype(BF16)
    _strip_loops(tm, per_stream, [sc_ref, sh_ref], bc_ref, strip)


def _prenorm(x, mod, layer, per_stream):
    m, d = x.shape
    tm = TM_EW
    sp = _Specs()
    in_specs = [
        sp.block((tm, d), lambda i: (i, 0), F32),
        _mod_block(sp, layer, MOD_SC_M, tm, d, per_stream, 1),
        _mod_block(sp, layer, MOD_SH_M, tm, d, per_stream, 1),
    ]
    out_spec = sp.block((tm, d), lambda i: (i, 0), BF16)
    return pl.pallas_call(
        functools.partial(_prenorm_kernel, tm=tm, per_stream=per_stream),
        grid=(m // tm,),
        in_specs=in_specs,
        out_specs=out_spec,
        out_shape=jax.ShapeDtypeStruct((m, d), BF16),
        scratch_shapes=[sp.scratch((2, V7X_SUBLANES, d), F32)],
        compiler_params=sp.params(("parallel",)),
        name="prenorm",
    )(x, mod, mod)


class _SideConvert:
    def __init__(self, src, layer, block):
        self.src, self.layer, self.block = src, layer, block
        rows, cols = src.shape[1:]
        br, bc = block
        assert rows % br == 0 and cols % bc == 0 and br % V7X_BF16_SUBLANES == 0 and bc % V7X_LANES == 0
        self.per_row = cols // bc
        self.n_blocks = (rows // br) * self.per_row

    def specs(self, sp, grid):
        n_steps = int(np.prod(grid))
        assert n_steps >= self.n_blocks
        last, per_row, layer = self.n_blocks - 1, self.per_row, self.layer

        def block_of(*idx):
            step = idx[0]
            for size, k in zip(grid[1:], idx[1:]):
                step = step * size + k
            b = jnp.minimum(step, last)
            return b // per_row, b % per_row
        in_spec = sp.block((None,) + self.block, lambda *idx: (layer,) + block_of(*idx), self.src.dtype)
        out_spec = sp.block(self.block, block_of, BF16)
        return in_spec, out_spec, jax.ShapeDtypeStruct(self.src.shape[1:], BF16)


def _side_specs(sp, side, grid):
    specs = [c.specs(sp, grid) for c in side]
    return [s[0] for s in specs], [s[1] for s in specs], [s[2] for s in specs], [c.src for c in side]


def _run_side_converts(refs, n_side):
    for src_ref, dst_ref in zip(refs[:n_side], refs[len(refs) - n_side:]):
        dst_ref[...] = src_ref[...].astype(BF16)


def _proj_kernel(h_ref, w_ref, *refs, emit_bf16, emit_f32, n_side):
    acc = jnp.dot(h_ref[...], w_ref[...].astype(BF16), preferred_element_type=F32)
    outs = list(refs[n_side:])
    if emit_bf16:
        outs.pop(0)[...] = acc.astype(BF16)
    if emit_f32:
        outs.pop(0)[...] = acc
    _run_side_converts(refs, n_side)


def _proj(h, w, layer, *, row0, m, col0, n, emit_bf16, emit_f32, tm, side=()):
    d = h.shape[1]
    tn = TN_PROJ
    assert row0 % tm == 0 and m % tm == 0 and col0 % tn == 0 and n % tn == 0
    rb, cb = row0 // tm, col0 // tn
    grid = (m // tm, n // tn)
    sp = _Specs()
    side_in, side_out, side_shape, side_args = _side_specs(sp, side, grid)
    in_specs = [
        sp.block((tm, d), lambda i, j: (i + rb, 0), BF16),
        sp.block((None, d, tn), lambda i, j: (layer, 0, j + cb), w.dtype),
    ] + side_in
    out_specs, out_shape = [], []
    if emit_bf16:
        out_specs.append(sp.block((tm, tn), lambda i, j: (i, j), BF16))
        out_shape.append(jax.ShapeDtypeStruct((m, n), BF16))
    if emit_f32:
        out_specs.append(sp.block((tm, tn), lambda i, j: (i, j), F32))
        out_shape.append(jax.ShapeDtypeStruct((m, n), F32))
    return pl.pallas_call(
        functools.partial(_proj_kernel, emit_bf16=emit_bf16, emit_f32=emit_f32, n_side=len(side)),
        grid=grid,
        in_specs=in_specs,
        out_specs=out_specs + side_out,
        out_shape=out_shape + side_shape,
        compiler_params=sp.params(("arbitrary", "arbitrary"),
                                  temp_bytes=_nbytes((tm, tn), F32) + _nbytes((d, tn), BF16)),
        name="in_proj",
    )(h, w, *side_args)


def _glu_kernel(h_ref, wa_ref, wb_ref, *refs, n_side):
    h = h_ref[...]
    a = jnp.dot(h, wa_ref[...].astype(BF16), preferred_element_type=F32)
    b = jnp.dot(h, wb_ref[...].astype(BF16), preferred_element_type=F32)
    refs[n_side][...] = a * _sigmoid(b)
    _run_side_converts(refs, n_side)


def _glu(h, w, layer, *, col_a, col_b, n, tm, side=()):
    m, d = h.shape
    tn = TN_GLU
    ca, cb = col_a // tn, col_b // tn
    grid = (m // tm, n // tn)
    sp = _Specs()
    side_in, side_out, side_shape, side_args = _side_specs(sp, side, grid)
    in_specs = [
        sp.block((tm, d), lambda i, j: (i, 0), BF16),
        sp.block((None, d, tn), lambda i, j: (layer, 0, j + ca), w.dtype),
        sp.block((None, d, tn), lambda i, j: (layer, 0, j + cb), w.dtype),
    ] + side_in
    out_spec = sp.block((tm, tn), lambda i, j: (i, j), F32)
    return pl.pallas_call(
        functools.partial(_glu_kernel, n_side=len(side)),
        grid=grid,
        in_specs=in_specs,
        out_specs=[out_spec] + side_out,
        out_shape=[jax.ShapeDtypeStruct((m, n), F32)] + side_shape,
        compiler_params=sp.params(("arbitrary", "arbitrary"),
                                  temp_bytes=3 * _nbytes((tm, tn), F32) + 2 * _nbytes((d, tn), BF16)),
        name="in_proj_glu",
    )(h, w, w, *side_args)


def _softmax_pv(parts):
    m = None
    for s, _ in parts:
        mi = jnp.max(s, axis=-1, keepdims=True)
        m = mi if m is None else jnp.maximum(m, mi)
    l, o = None, None
    for s, v in parts:
        p = jnp.exp2(s - m)
        li = jnp.sum(p, axis=-1, keepdims=True)
        oi = jnp.dot(p.astype(BF16), v, preferred_element_type=F32)
        l = li if l is None else l + li
        o = oi if o is None else o + oi
    return o / l


def _qk(q, k):
    return lax.dot_general(q, k, (((1,), (1,)), ((), ())), preferred_element_type=F32)


def _attn_prompt_kernel(q_ref, k_ref, v_ref, strip_ref, *refs, scale, hd, heads, n_side):
    g = pl.program_id(1)
    rows = Q_CHUNKS * CHUNK
    window = BAND_CHUNKS * CHUNK
    o_ref, b_ref = refs[n_side], refs[-1]
    _run_side_converts(refs[:-1], n_side)

    @pl.when(g == 0)
    def _():
        for h in range(heads):
            for qi in range(Q_CHUNKS):
                off = (Q_CHUNKS - 1 - qi) * CHUNK
                b_ref[h, qi * CHUNK:(qi + 1) * CHUNK, :] = strip_ref[h, :, off:off + window]

    def run(k0, n_keys, b_off):
        for h in range(heads):
            c = slice(h * hd, (h + 1) * hd)
            k = k_ref[pl.ds(k0, n_keys), c]
            v = v_ref[pl.ds(k0, n_keys), c]
            s = _qk(q_ref[:, c], k) * scale + b_ref[h, :, b_off:b_off + n_keys]
            o_ref[:, c] = _softmax_pv([(s, v)])

    n_lead = N_PAST_CHUNKS // Q_CHUNKS
    for lead in range(n_lead):
        keys = (lead + 1) * rows
        pl.when(g == lead)(functools.partial(run, 0, keys, window - keys))

    @pl.when(g >= n_lead)
    def _():
        run(pl.multiple_of((g - n_lead) * rows, rows), window, 0)


def _attn_prompt(qkv, bias, layer, n_heads, hd, side=()):
    t = qkv.shape[0]
    rows = Q_CHUNKS * CHUNK
    keys = BAND_CHUNKS * CHUNK
    heads = ATTN_HEADS_PER_STEP
    assert n_heads % heads == 0
    n_groups = n_heads // heads
    width = heads * hd
    grid = (n_groups, t // rows)
    sp = _Specs()
    side_in, side_out, side_shape, side_args = _side_specs(sp, side, grid)
    in_specs = [
        sp.block((rows, width), lambda h, g: (g, h), BF16),
        sp.block((t, width), lambda h, g: (0, n_groups + h), BF16),
        sp.block((t, width), lambda h, g: (0, 2 * n_groups + h), BF16),
        sp.block((None, heads, CHUNK, bias.shape[-1]), lambda h, g: (layer, h, 0, 0), F32),
    ] + side_in
    out_spec = sp.block((rows, width), lambda h, g: (g, h), F32)
    return pl.pallas_call(
        functools.partial(_attn_prompt_kernel, scale=hd ** -0.5 * LOG2E, hd=hd, heads=heads, n_side=len(side)),
        grid=grid,
        in_specs=in_specs,
        out_specs=[out_spec] + side_out,
        out_shape=[jax.ShapeDtypeStruct((t, n_heads * hd), F32)] + side_shape,
        scratch_shapes=[sp.scratch((heads, rows, keys), F32)],
        compiler_params=sp.params(("arbitrary", "arbitrary"), temp_bytes=6 * _nbytes((rows, keys), F32)),
        name="attn_prompt",
    )(qkv, qkv, qkv, bias, *side_args)


def _attn_sample_kernel(q_ref, kn_ref, vn_ref, kc_ref, vc_ref, bc_ref, bn_ref, o_ref, *, scale, n_heads, hd):
    keep = kc_ref.shape[0] // n_heads
    for h in range(n_heads):
        c = slice(h * hd, (h + 1) * hd)
        frames = pl.ds(h, keep, stride=n_heads)
        q = q_ref[:, c]
        s_cache = _qk(q, kc_ref[frames, :].astype(BF16)) * scale + bc_ref[h]
        s_new = _qk(q, kn_ref[:, c]) * scale + bn_ref[h]
        o_ref[:, c] = _softmax_pv([(s_cache, vc_ref[frames, :].astype(BF16)), (s_new, vn_ref[:, c])])


def _attn_sample(q, kv, cache_k, cache_v, bias_cache, bias_new, layer, n_heads, hd):
    m, da = q.shape
    n_streams = m // CHUNK
    keep = cache_k.shape[2] // n_heads
    sp = _Specs()
    in_specs = [
        sp.block((CHUNK, da), lambda b: (b, 0), BF16),
        sp.block((CHUNK, da), lambda b: (b, 0), BF16),
        sp.block((CHUNK, da), lambda b: (b, 1), BF16),
        sp.block((None, None, keep * n_heads, hd), lambda b: (layer, b, 0, 0), F32),
        sp.block((None, None, keep * n_heads, hd), lambda b: (layer, b, 0, 0), F32),
        sp.block((None, n_heads, CHUNK, keep), lambda b: (layer, 0, 0, 0), F32),
        sp.block((None, n_heads, CHUNK, CHUNK), lambda b: (layer, 0, 0, 0), F32),
    ]
    out_spec = sp.block((CHUNK, da), lambda b: (b, 0), F32)
    return pl.pallas_call(
        functools.partial(_attn_sample_kernel, scale=hd ** -0.5 * LOG2E, n_heads=n_heads, hd=hd),
        grid=(n_streams,),
        in_specs=in_specs,
        out_specs=out_spec,
        out_shape=jax.ShapeDtypeStruct((m, da), F32),
        compiler_params=sp.params(("parallel",), temp_bytes=4 * 1024 * 1024),
        name="attn_sample",
    )(q, kv, kv, cache_k, cache_v, bias_cache, bias_new)


def _bias_tables(rel_bias):
    depth, n_heads, n_rel = rel_bias.shape
    max_rel = (n_rel - 1) // 2
    keep = N_PAST_CHUNKS * CHUNK
    lead = (Q_CHUNKS - 1) * CHUNK
    cols = BAND_CHUNKS * CHUNK
    width = lead + cols
    p = CHUNK + width - 1
    x = np.arange(p)
    rel = np.where(x < width, -x, p - x) + keep + lead
    diag = rel_bias[:, :, np.clip(rel, -max_rel, max_rel) + max_rel]
    a = jnp.tile(diag, (1, 1, CHUNK))[:, :, :CHUNK * (p - 1)].reshape(depth, n_heads, CHUNK, p - 1)
    strip = a[..., :width] * LOG2E
    col = np.arange(width)
    masked = jnp.where((col >= lead) & (col < lead + keep + CHUNK), strip, NEG_INF)
    lane_pad = -width % V7X_LANES
    masked = jnp.pad(masked, ((0, 0), (0, 0), (0, 0), (0, lane_pad)), constant_values=NEG_INF)
    return masked, strip[..., lead:lead + keep], strip[..., lead + keep:lead + keep + CHUNK]


def _conv_cat_kernel(u_ref, hist_ref, attn_ref, wdw_ref, bdw_ref, gln_ref, bln_ref, gao_ref, gco_ref,
                     *refs, tt, conv_width, hist_is_state, n_side):
    xs_ref, y_ref, bc_ref = refs[-3:]
    cat_ref = refs[n_side]
    _run_side_converts(refs[:-3], n_side)
    da = attn_ref.shape[1]
    dc = u_ref.shape[1]
    if hist_is_state:
        n_hist = hist_ref.shape[0]
        xs_ref[0, 0:V7X_SUBLANES, :] = jnp.zeros((V7X_SUBLANES, dc), F32)
        xs_ref[0, HALO - n_hist:HALO, :] = hist_ref[...]
    else:
        xs_ref[0, 0:HALO, :] = jnp.where(pl.program_id(0) > 0, hist_ref[...], 0.0)
    xs_ref[0, HALO:HALO + tt, :] = u_ref[...]
    first = HALO - (conv_width - 1)
    n_shift = HALO + tt - V7X_SUBLANES
    for r in range(1, V7X_SUBLANES):
        xs_ref[r, 0:n_shift, :] = xs_ref[0, r:r + n_shift, :]
    rc_rows = 64
    for rc in range(tt // rc_rows):
        for cb in range(dc // V7X_LANES):
            c = slice(cb * V7X_LANES, (cb + 1) * V7X_LANES)
            acc = None
            for w in range(conv_width):
                r0 = rc * rc_rows + (first + w) // V7X_SUBLANES * V7X_SUBLANES
                term = xs_ref[(first + w) % V7X_SUBLANES, r0:r0 + rc_rows, c] * wdw_ref[w:w + 1, c]
                acc = term if acc is None else acc + term
            y_ref[rc * rc_rows:(rc + 1) * rc_rows, c] = acc + bdw_ref[:, c]

    for k, ref in enumerate((gln_ref, bln_ref, gco_ref, gao_ref)):
        bc_ref[k] = jnp.broadcast_to(ref[...], bc_ref.shape[1:])

    def strip(j, carry):
        r = pl.ds(pl.multiple_of(j * CONV_STRIP, CONV_STRIP), CONV_STRIP)
        y = _split(y_ref[r, :])
        yc = y - jnp.mean(y, axis=-1, keepdims=True)
        yn = yc * _inv_rms(yc) * bc_ref[0] + bc_ref[1]
        z = yn * _sigmoid(yn)
        cat_ref[r, da:da + dc] = _merge(z * _inv_rms(z) * bc_ref[2]).astype(BF16)
        a = _split(attn_ref[r, :])
        cat_ref[r, 0:da] = _merge(a * _inv_rms(a) * bc_ref[3]).astype(BF16)
        return carry
    n_strips = tt // CONV_STRIP
    lax.fori_loop(0, n_strips, strip, 0, unroll=2 if n_strips > 2 else 1)


def _conv_cat(u, state, attn, w_dw, b_dw, g_ln, b_ln, g_ao, g_co, layer, tt, side=()):
    m, dc = u.shape
    da = attn.shape[1]
    conv_width = w_dw.shape[1]
    assert conv_width - 1 <= HALO and tt % HALO == 0
    assert da == dc
    grid = (m // tt,)
    sp = _Specs()
    side_in, side_out, side_shape, side_args = _side_specs(sp, side, grid)
    vec = lambda width: sp.block((None, 1, width), lambda i: (layer, 0, 0), F32)
    if state is None:
        hist, hist_spec = u, sp.block((HALO, dc), lambda i: (jnp.maximum(i * (tt // HALO) - 1, 0), 0), F32)
    else:
        assert state.shape[1:] == (m // tt, conv_width - 1, dc)
        hist, hist_spec = state, sp.block((None, None, conv_width - 1, dc), lambda i: (layer, i, 0, 0), F32)
    in_specs = [
        sp.block((tt, dc), lambda i: (i, 0), F32),
        hist_spec,
        sp.block((tt, da), lambda i: (i, 0), F32),
        sp.block((None, conv_width, dc), lambda i: (layer, 0, 0), F32),
        vec(dc), vec(dc), vec(dc), vec(da), vec(dc),
    ] + side_in
    out_spec = sp.block((tt, da + dc), lambda i: (i, 0), BF16)
    scratch = [sp.scratch((V7X_SUBLANES, HALO + tt, dc), F32), sp.scratch((tt, dc), F32),
               sp.scratch((4, V7X_SUBLANES, dc), F32)]
    return pl.pallas_call(
        functools.partial(_conv_cat_kernel, tt=tt, conv_width=conv_width, hist_is_state=state is not None,
                          n_side=len(side)),
        grid=grid,
        in_specs=in_specs,
        out_specs=[out_spec] + side_out,
        out_shape=[jax.ShapeDtypeStruct((m, da + dc), BF16)] + side_shape,
        scratch_shapes=scratch,
        compiler_params=sp.params(("arbitrary",), temp_bytes=4 * _nbytes((tt, dc), F32)),
        name="conv_cat",
    )(u, hist, attn, w_dw, b_dw, g_ln, b_ln, g_ao, g_co, *side_args)


def _outproj_kernel(cat_ref, w_ref, x_hbm, gt_ref, sc_ref, sh_ref, x1_ref, h2_ref, xbuf_ref, bc_ref, in_sem,
                    *, tm, nk, per_stream):
    i = pl.program_id(0)
    k = pl.program_id(1)
    n_chunks = tm // FFN_CHUNK
    strips = FFN_CHUNK // STRIP

    def x_copy(c, slot):
        rows = pl.ds(pl.multiple_of(i * tm + c * FFN_CHUNK, FFN_CHUNK), FFN_CHUNK)
        return pltpu.make_async_copy(x_hbm.at[rows, :], xbuf_ref.at[slot], in_sem.at[slot])

    @pl.when(k == nk - 1)
    def _():
        x_copy(0, 0).start()
        x_copy(1, 1).start()

    @pl.when(k == 0)
    def _():
        x1_ref[...] = jnp.dot(cat_ref[...], w_ref[...], preferred_element_type=F32)

    @pl.when(k > 0)
    def _():
        x1_ref[...] += jnp.dot(cat_ref[...], w_ref[...], preferred_element_type=F32)

    @pl.when(k == nk - 1)
    def _():
        def vectors(row):
            for j, ref in enumerate((gt_ref, sc_ref, sh_ref)):
                bc_ref[j] = jnp.broadcast_to(ref[pl.ds(row, 1), :], bc_ref.shape[1:])
        if not per_stream:
            vectors(0)

        def chunk(c, carry):
            slot = c % 2
            x_copy(c, slot).wait()
            for t in range(strips):
                if per_stream:
                    vectors(c * strips + t)
                r = pl.ds(pl.multiple_of(c * FFN_CHUNK + t * STRIP, STRIP), STRIP)
                mixed = _split(x1_ref[r, :])
                x1 = _split(xbuf_ref[slot, t * STRIP:(t + 1) * STRIP, :]) + mixed * _inv_rms(mixed) * bc_ref[0]
                x1_ref[r, :] = _merge(x1)
                h2_ref[r, :] = _merge(x1 * _inv_rms(x1) * bc_ref[1] + bc_ref[2]).astype(BF16)

            @pl.when(c + 2 < n_chunks)
            def _():
                x_copy(c + 2, slot).start()
            return carry
        lax.fori_loop(0, n_chunks, chunk, 0)


def _outproj(cat, w_out, x, mod, layer, per_stream, tm):
    m, dmix = cat.shape
    d = w_out.shape[1]
    tk = TK_OUT
    nk = dmix // tk
    assert tm % (2 * FFN_CHUNK) == 0 and FFN_CHUNK % STRIP == 0 and (STRIP == CHUNK or not per_stream)
    sp = _Specs()
    in_specs = [
        sp.block((tm, tk), lambda i, k: (i, k), BF16),
        sp.block((tk, d), lambda i, k: (k, 0), BF16),
        pl.BlockSpec(memory_space=pl.ANY),
        _mod_block(sp, layer, MOD_GT_M, tm, d, per_stream, 2),
        _mod_block(sp, layer, MOD_SC_F, tm, d, per_stream, 2),
        _mod_block(sp, layer, MOD_SH_F, tm, d, per_stream, 2),
    ]
    out_specs = [
        sp.block((tm, d), lambda i, k: (i, 0), F32),
        sp.block((tm, d), lambda i, k: (i, 0), BF16),
    ]
    return pl.pallas_call(
        functools.partial(_outproj_kernel, tm=tm, nk=nk, per_stream=per_stream),
        grid=(m // tm, nk),
        in_specs=in_specs,
        out_specs=out_specs,
        out_shape=[jax.ShapeDtypeStruct((m, d), F32), jax.ShapeDtypeStruct((m, d), BF16)],
        scratch_shapes=[sp.scratch((2, FFN_CHUNK, d), F32), sp.scratch((3, V7X_SUBLANES, d), F32),
                        pltpu.SemaphoreType.DMA((2,))],
        compiler_params=sp.params(("arbitrary", "arbitrary")),
        name="out_proj",
    )(cat, w_out, x, mod, mod, mod)


def _ffn_kernel(h_ref, wg_ref, wu_ref, wd_ref, x1_hbm, gt_ref, x2_hbm, acc_ref, xbuf_ref, bc_ref, in_sem, out_sem,
                *, tm, nf, per_stream):
    i = pl.program_id(0)
    f = pl.program_id(1)
    n_chunks = tm // FFN_CHUNK
    strips = FFN_CHUNK // STRIP

    def chunk_rows(c):
        return pl.ds(pl.multiple_of(i * tm + c * FFN_CHUNK, FFN_CHUNK), FFN_CHUNK)

    def x1_copy(c, slot):
        return pltpu.make_async_copy(x1_hbm.at[chunk_rows(c), :], xbuf_ref.at[slot], in_sem.at[slot])

    def out_copy(c):
        src = acc_ref.at[pl.ds(pl.multiple_of(c * FFN_CHUNK, FFN_CHUNK), FFN_CHUNK), :]
        return pltpu.make_async_copy(src, x2_hbm.at[chunk_rows(c), :], out_sem.at[0])

    @pl.when(f == 0)
    def _():
        acc_ref[...] = jnp.zeros_like(acc_ref)

    @pl.when(f == nf - 1)
    def _():
        x1_copy(0, 0).start()
        x1_copy(1, 1).start()

    h = h_ref[...]
    g = jnp.dot(h, wg_ref[...], preferred_element_type=F32)
    u = jnp.dot(h, wu_ref[...], preferred_element_type=F32)
    a = (g * _sigmoid(g) * u).astype(BF16)
    acc_ref[...] += jnp.dot(a, wd_ref[...], preferred_element_type=F32)

    @pl.when(f == nf - 1)
    def _():
        def gate(row):
            bc_ref[0] = jnp.broadcast_to(gt_ref[pl.ds(row, 1), :], bc_ref.shape[1:])
        if not per_stream:
            gate(0)

        def chunk(c, carry):
            slot = c % 2
            x1_copy(c, slot).wait()
            for t in range(strips):
                if per_stream:
                    gate(c * strips + t)
                r = pl.ds(pl.multiple_of(c * FFN_CHUNK + t * STRIP, STRIP), STRIP)
                y = _split(acc_ref[r, :])
                x1 = _split(xbuf_ref[slot, t * STRIP:(t + 1) * STRIP, :])
                acc_ref[r, :] = _merge(x1 + y * _inv_rms(y) * bc_ref[0])
            out_copy(c).start()

            @pl.when(c + 2 < n_chunks)
            def _():
                x1_copy(c + 2, slot).start()
            return carry
        lax.fori_loop(0, n_chunks, chunk, 0)
        for c in range(n_chunks):
            out_copy(c).wait()


def _ffn(h2, wg, wu, wd, x1, mod, layer, per_stream, tm):
    m, d = h2.shape
    dff = wg.shape[1]
    tf = TF_FFN
    nf = dff // tf
    assert dff % tf == 0 and tm % (2 * FFN_CHUNK) == 0 and FFN_CHUNK % STRIP == 0
    assert STRIP == CHUNK or not per_stream
    sp = _Specs()
    in_specs = [
        sp.block((tm, d), lambda i, f: (i, 0), BF16),
        sp.block((d, tf), lambda i, f: (0, f), BF16),
        sp.block((d, tf), lambda i, f: (0, f), BF16),
        sp.block((tf, d), lambda i, f: (f, 0), BF16),
        pl.BlockSpec(memory_space=pl.ANY),
        _mod_block(sp, layer, MOD_GT_F, tm, d, per_stream, 2),
    ]
    scratch = [
        sp.scratch((tm, d), F32),
        sp.scratch((2, FFN_CHUNK, d), F32),
        sp.scratch((1, V7X_SUBLANES, d), F32),
        pltpu.SemaphoreType.DMA((2,)),
        pltpu.SemaphoreType.DMA((1,)),
    ]
    return pl.pallas_call(
        functools.partial(_ffn_kernel, tm=tm, nf=nf, per_stream=per_stream),
        grid=(m // tm, nf),
        in_specs=in_specs,
        out_specs=pl.BlockSpec(memory_space=pl.ANY),
        out_shape=jax.ShapeDtypeStruct((m, d), F32),
        scratch_shapes=scratch,
        compiler_params=sp.params(("arbitrary", "arbitrary"), temp_bytes=4 * _nbytes((tm, tf), F32)),
        name="ffn",
    )(h2, wg, wu, wd, x1, mod)


def kernel(x_prompt, x_sample, cache_k, cache_v, state_conv, c_prompt, c_sample, w_ada, b_ada, g_mix_pre,
           g_mix_post, w_in, rel_bias, w_dw, b_dw, g_conv_ln, b_conv_ln, g_attn_out, g_conv_out, w_out,
           g_ffn_pre, g_ffn_post, w_ffn_gate, w_ffn_up, w_ffn_down):
    batch, seq, d = x_prompt.shape
    dec_batch, dec_seq, _ = x_sample.shape
    depth, _, keep, n_heads, hd = cache_k.shape
    da = n_heads * hd
    dc = state_conv.shape[3]
    conv_state = state_conv.shape[2]
    assert batch == 1 and dec_seq == CHUNK and keep == N_PAST_CHUNKS * CHUNK and dec_batch == PROMPT_MOD_ROW
    assert seq % (Q_CHUNKS * CHUNK) == 0 and seq >= keep
    m_s = dec_batch * dec_seq

    vec = lambda a: a.reshape(depth, 1, a.shape[-1])
    g_mix_pre, g_mix_post, g_ffn_pre, g_ffn_post = map(vec, (g_mix_pre, g_mix_post, g_ffn_pre, g_ffn_post))
    b_dw, g_conv_ln, b_conv_ln, g_attn_out, g_conv_out = map(vec, (b_dw, g_conv_ln, b_conv_ln, g_attn_out, g_conv_out))

    c_all = jnp.concatenate(
        [c_sample, c_prompt, jnp.zeros((MOD_ROWS - dec_batch - batch, d), F32)], axis=0)
    mod = _ada(c_all, w_ada, b_ada, g_mix_pre, g_mix_post, g_ffn_pre, g_ffn_post)

    dff = w_ffn_gate.shape[2]
    band_bias, cache_bias, new_bias = _bias_tables(rel_bias)
    cache_k2 = cache_k.reshape(depth, dec_batch, keep * n_heads, hd)
    cache_v2 = cache_v.reshape(depth, dec_batch, keep * n_heads, hd)

    xp = x_prompt.reshape(seq, d)
    xs = x_sample.reshape(m_s, d)
    outs = {k: [] for k in ("conv_p", "k_p", "v_p", "conv_s", "k_s", "v_s")}
    tt_p = TT_CONV_PROMPT
    for l in range(depth):
        h = _prenorm(xp, mod, l, per_stream=False)
        qkv, wd_b = _proj(h, w_in, l, row0=0, m=seq, col0=0, n=3 * da, emit_bf16=True, emit_f32=False, tm=TM,
                          side=[_SideConvert(w_ffn_down, l, (dff // SIDE_ROW_BLOCKS, d // SIDE_COL_BLOCKS))])
        (kv_tail,) = _proj(h, w_in, l, row0=seq - keep, m=keep, col0=da, n=2 * da,
                           emit_bf16=False, emit_f32=True, tm=keep)
        (u,) = _glu(h, w_in, l, col_a=3 * da, col_b=3 * da + dc, n=dc, tm=TM)
        attn, wu_b, w_out_b = _attn_prompt(qkv, band_bias, l, n_heads, hd,
                                           side=[_SideConvert(w_ffn_up, l, (d // ATTN_SIDE_BLOCKS, dff)),
                                                 _SideConvert(w_out, l, (w_out.shape[1] // ATTN_SIDE_BLOCKS, d))])
        cat, wg_b = _conv_cat(u, None, attn, w_dw, b_dw, g_conv_ln, b_conv_ln, g_attn_out, g_conv_out, l, tt_p,
                              side=[_SideConvert(w_ffn_gate, l, (d // SIDE_BLOCKS, dff))])
        x1, h2 = _outproj(cat, w_out_b, xp, mod, l, False, TM_OUT)
        xp = _ffn(h2, wg_b, wu_b, wd_b, x1, mod, l, False, TM)
        outs["conv_p"].append(u[seq - conv_state:].reshape(batch, conv_state, dc))
        outs["k_p"].append(kv_tail[:, :da].reshape(batch, keep, n_heads, hd))
        outs["v_p"].append(kv_tail[:, da:].reshape(batch, keep, n_heads, hd))

        h = _prenorm(xs, mod, l, per_stream=True)
        (q_s,) = _proj(h, w_in, l, row0=0, m=m_s, col0=0, n=da, emit_bf16=True, emit_f32=False, tm=TM)
        kv_s, kv_s32 = _proj(h, w_in, l, row0=0, m=m_s, col0=da, n=2 * da, emit_bf16=True, emit_f32=True, tm=TM)
        (u,) = _glu(h, w_in, l, col_a=3 * da, col_b=3 * da + dc, n=dc, tm=TM)
        attn = _attn_sample(q_s, kv_s, cache_k2, cache_v2, cache_bias, new_bias, l, n_heads, hd)
        (cat,) = _conv_cat(u, state_conv, attn, w_dw, b_dw, g_conv_ln, b_conv_ln, g_attn_out, g_conv_out, l, CHUNK)
        x1, h2 = _outproj(cat, w_out_b, xs, mod, l, True, TM_OUT)
        xs = _ffn(h2, wg_b, wu_b, wd_b, x1, mod, l, True, TM)
        outs["conv_s"].append(u.reshape(dec_batch, dec_seq, dc)[:, dec_seq - conv_state:, :])
        outs["k_s"].append(kv_s32[:, :da].reshape(dec_batch, dec_seq, n_heads, hd))
        outs["v_s"].append(kv_s32[:, da:].reshape(dec_batch, dec_seq, n_heads, hd))

    stack = lambda name: jnp.stack(outs[name])
    return (xp.reshape(batch, seq, d), xs.reshape(dec_batch, dec_seq, d),
            stack("conv_p"), stack("k_p"), stack("v_p"), stack("conv_s"), stack("k_s"), stack("v_s"))
```
